```python
import math
import jax, jax.numpy as jnp
from jax import lax
import numpy as np

D_MODEL = 2048
BATCH = 4
SEQ = 2048
DEPTH = 4
DEC_BATCH = 8
DEC_SEQ = 1
PAST_LEN = 16384
PAGE_SIZE = 128

N_MIXERS = 2
N_HGRN = (DEPTH + 1) // 2
N_MOBA = DEPTH // 2
BRANCH = D_MODEL
IN_COLS = 4 * BRANCH
HG_HEADS = 16
HG_DK = 128
HG_DV = BRANCH // HG_HEADS
HG_FDIM = HG_HEADS * HG_DK
HG_CHUNK = 64
AT_HEADS = 16
AT_HD = BRANCH // AT_HEADS
MOBA_BLOCK = 256
MOBA_TOPK = 3
MOBA_QCHUNK = 8
ROPE_THETA = 10000.0
RMS_EPS = 1e-6
NEG = -1e30
LB_FLOOR = 1e-30

kernel_name = "hgrn2_moba_interleaved_decoder_step"

F32 = jnp.float32


def rmsnorm(x, w):
    xf = x.astype(F32)
    y = xf * lax.rsqrt(jnp.mean(xf * xf, axis=-1, keepdims=True) + RMS_EPS)
    return (y * w.astype(F32)).astype(x.dtype)


def rope(x, pos):
    half = x.shape[-1] // 2
    inv = jnp.power(ROPE_THETA, -jnp.arange(half, dtype=F32) / half)
    ang = pos[:, None] * inv[None, :]
    cos = jnp.cos(ang)[None, :, None, :]
    sin = jnp.sin(ang)[None, :, None, :]
    xf = x.astype(F32)
    x1, x2 = xf[..., :half], xf[..., half:]
    return jnp.concatenate([x1 * cos - x2 * sin, x2 * cos + x1 * sin], axis=-1).astype(x.dtype)


def hgrn_gates(f_pre, lb):
    z = f_pre.astype(F32)
    log_lb = jnp.log(jnp.maximum(lb, LB_FLOOR))
    log_f = jnp.logaddexp(log_lb, jnp.log1p(-lb) + jax.nn.log_sigmoid(z))
    k = (1.0 - lb) * jax.nn.sigmoid(-z)
    return log_f, k


def hgrn_chunk_scan(q, k, v, log_f, s0):
    B, L = q.shape[:2]
    c = HG_CHUNK if L % HG_CHUNK == 0 else L
    n = L // c

    def to_chunks(t):
        return t.astype(F32).reshape(B, n, c, HG_HEADS, t.shape[-1]).transpose(1, 0, 3, 2, 4)

    causal = jnp.tril(jnp.ones((c, c), dtype=bool))[:, :, None]

    def step(S, inp):
        qc, kc, vc, gc = inp
        b = jnp.cumsum(gc, axis=-2)
        diff = b[:, :, :, None, :] - b[:, :, None, :, :]
        decay = jnp.where(causal, jnp.exp(jnp.where(causal, diff, 0.0)), 0.0)
        A = jnp.einsum('bhtd,bhsd,bhtsd->bhts', qc, kc, decay)
        o = jnp.einsum('bhts,bhsv->bhtv', A, vc) + jnp.einsum('bhtd,bhdv->bhtv', qc * jnp.exp(b), S)
        b_last = b[:, :, -1:, :]
        S_new = jnp.exp(b_last[:, :, 0, :])[..., None] * S + jnp.einsum(
            'bhsd,bhsv->bhdv', kc * jnp.exp(b_last - b), vc)
        return S_new, o

    S_fin, o = lax.scan(step, s0.astype(F32), (to_chunks(q), to_chunks(k), to_chunks(v), to_chunks(log_f)))
    o = o.transpose(1, 0, 3, 2, 4).reshape(B, L, HG_HEADS, HG_DV)
    return o, S_fin


def hgrn_branch(proj, lb, gnorm, s0):
    B, L = proj.shape[:2]
    qa, fa, ia, z = jnp.split(proj, 4, axis=-1)
    q = jax.nn.silu(qa).reshape(B, L, HG_HEADS, HG_DK)
    log_f, k = hgrn_gates(fa, lb)
    log_f = log_f.reshape(B, L, HG_HEADS, HG_DK)
    k = k.reshape(B, L, HG_HEADS, HG_DK)
    v = ia.reshape(B, L, HG_HEADS, HG_DV)
    o, S = hgrn_chunk_scan(q, k, v, log_f, s0)
    o = o * lax.rsqrt(jnp.mean(o * o, axis=-1, keepdims=True) + RMS_EPS) * gnorm.astype(F32)
    y = o.reshape(B, L, BRANCH).astype(proj.dtype) * jax.nn.silu(z)
    return y, S


def moba_select(q, block_means, q_block):
    n_blocks = block_means.shape[1]
    k_sel = min(MOBA_TOPK, n_blocks)
    s = jnp.einsum('bqhd,bnhd->bqhn', q.astype(F32), block_means.astype(F32))
    fully_past = jnp.arange(n_blocks)[None, None, None, :] < q_block[None, :, None, None]
    s = jnp.where(fully_past, s, NEG)
    _, idx = lax.top_k(s, k_sel)
    valid = idx < q_block[None, :, None, None]
    own = jnp.broadcast_to(q_block[None, :, None, None], idx.shape[:-1] + (1,)).astype(idx.dtype)
    blocks = jnp.concatenate([idx, own], axis=-1)
    slot_ok = jnp.concatenate([valid, jnp.ones(own.shape, dtype=bool)], axis=-1)
    return blocks, slot_ok


def moba_attend(q, kg, vg, blocks, slot_ok, q_pos):
    key_pos = blocks[..., None] * MOBA_BLOCK + jnp.arange(MOBA_BLOCK, dtype=blocks.dtype)
    mask = slot_ok[..., None] & (key_pos <= q_pos[None, :, None, None, None])
    s = jnp.einsum('bqhd,bqhnkd->bqhnk', q.astype(F32), kg.astype(F32)) * (AT_HD ** -0.5)
    B, Q, H, NS, BS = s.shape
    s = jnp.where(mask, s, NEG).reshape(B, Q, H, NS * BS)
    p = jax.nn.softmax(s, axis=-1).reshape(B, Q, H, NS, BS)
    p = jnp.where(mask, p, 0.0)
    o = jnp.einsum('bqhnk,bqhnkd->bqhd', p, vg.astype(F32))
    return o.astype(q.dtype)


def moba_prompt(q, k, v):
    B, S = q.shape[:2]
    nb = -(-S // MOBA_BLOCK)
    pad = nb * MOBA_BLOCK - S
    kb = jnp.pad(k, ((0, 0), (0, pad), (0, 0), (0, 0))).reshape(B, nb, MOBA_BLOCK, AT_HEADS, AT_HD)
    vb = jnp.pad(v, ((0, 0), (0, pad), (0, 0), (0, 0))).reshape(B, nb, MOBA_BLOCK, AT_HEADS, AT_HD)
    means = jnp.mean(kb.astype(F32), axis=2)
    pos = jnp.arange(S, dtype=jnp.int32)
    blocks, slot_ok = moba_select(q, means, pos // MOBA_BLOCK)
    qc = MOBA_QCHUNK if S % MOBA_QCHUNK == 0 else S
    nchunk = S // qc

    def chunk(t):
        return jnp.moveaxis(t.reshape((B, nchunk, qc) + t.shape[2:]), 1, 0)

    bidx = jnp.arange(B)[:, None, None, None, None]
    hidx = jnp.arange(AT_HEADS)[None, None, :, None, None]
    off = jnp.arange(MOBA_BLOCK)

    def body(inp):
        q_c, b_c, ok_c, pos_c = inp
        kg = kb[bidx, b_c[..., None], off, hidx]
        vg = vb[bidx, b_c[..., None], off, hidx]
        return moba_attend(q_c, kg, vg, b_c, ok_c, pos_c)

    o = lax.map(body, (chunk(q), chunk(blocks), chunk(slot_ok), pos.reshape(nchunk, qc)))
    return jnp.moveaxis(o, 0, 1).reshape(B, S, AT_HEADS, AT_HD)


def moba_sample(q, k_new, v_new, k_pool, v_pool, page_table):
    DB, Q = q.shape[:2]
    n_pages = page_table.shape[1]
    past = n_pages * PAGE_SIZE
    ppb = MOBA_BLOCK // PAGE_SIZE
    npn = -(-Q // PAGE_SIZE)
    padq = npn * PAGE_SIZE - Q
    kn = jnp.pad(k_new, ((0, 0), (0, padq), (0, 0), (0, 0))).reshape(DB, npn, PAGE_SIZE, AT_HEADS, AT_HD)
    vn = jnp.pad(v_new, ((0, 0), (0, padq), (0, 0), (0, 0))).reshape(DB, npn, PAGE_SIZE, AT_HEADS, AT_HD)
    page_sums = jnp.concatenate(
        [jnp.sum(k_pool, axis=1, dtype=F32)[page_table], jnp.sum(kn, axis=2, dtype=F32)], axis=1)
    nlp = n_pages + npn
    nbt = -(-nlp // ppb)
    page_sums = jnp.pad(page_sums, ((0, 0), (0, nbt * ppb - nlp), (0, 0), (0, 0)))
    means = page_sums.reshape(DB, nbt, ppb, AT_HEADS, AT_HD).sum(axis=2) / MOBA_BLOCK
    pos = past + jnp.arange(Q, dtype=jnp.int32)
    blocks, slot_ok = moba_select(q, means, pos // MOBA_BLOCK)
    lp = blocks[..., None] * ppb + jnp.arange(ppb, dtype=blocks.dtype)
    bidx = jnp.arange(DB)[:, None, None, None, None]
    phys = page_table[bidx, jnp.clip(lp, 0, n_pages - 1)]
    newp = jnp.clip(lp - n_pages, 0, npn - 1)
    in_cache = (lp < n_pages)[..., None, None]
    bidx6 = bidx[..., None]
    hidx = jnp.arange(AT_HEADS)[None, None, :, None, None, None]
    off = jnp.arange(PAGE_SIZE)

    def gather_rows(pool, new):
        from_cache = pool[phys[..., None], off, hidx]
        from_new = new[bidx6, newp[..., None], off, hidx]
        return jnp.where(in_cache, from_cache, from_new).reshape(DB, Q, AT_HEADS, -1, MOBA_BLOCK, AT_HD)

    return moba_attend(q, gather_rows(k_pool, kn), gather_rows(v_pool, vn), blocks, slot_ok, pos)


def moba_qkv(proj, pos):
    B, L = proj.shape[:2]
    qa, ka, va, z = jnp.split(proj, 4, axis=-1)
    q = rope(qa.reshape(B, L, AT_HEADS, AT_HD), pos)
    k = rope(ka.reshape(B, L, AT_HEADS, AT_HD), pos)
    v = va.reshape(B, L, AT_HEADS, AT_HD)
    return q, k, v, z


def setup_inputs(seed: int = 0) -> dict:
    key = jax.random.key(seed)
    ks = jax.random.split(key, 13)
    n_pages = PAST_LEN // PAGE_SIZE
    n_used = DEC_BATCH * n_pages
    n_pool = n_used + n_used // 4
    x_prompt = jax.random.normal(ks[0], (BATCH, SEQ, D_MODEL), F32)
    x_sample = jax.random.normal(ks[1], (DEC_BATCH, DEC_SEQ, D_MODEL), F32)
    cache_k = jax.random.normal(ks[2], (N_MOBA, n_pool, PAGE_SIZE, AT_HEADS, AT_HD), F32)
    cache_v = jax.random.normal(ks[3], (N_MOBA, n_pool, PAGE_SIZE, AT_HEADS, AT_HD), F32)
    state_hgrn = 0.5 * jax.random.normal(ks[4], (N_HGRN, DEC_BATCH, HG_HEADS, HG_DK, HG_DV), F32)
    page_table = jax.random.permutation(ks[5], n_pool)[:n_used].reshape(DEC_BATCH, n_pages).astype(jnp.int32)
    norm_w = 1.0 + 0.1 * jax.random.normal(ks[6], (DEPTH, D_MODEL), F32)
    w_in = jax.random.normal(ks[7], (DEPTH, D_MODEL, IN_COLS), F32) * (D_MODEL ** -0.5)
    w_out = jax.random.normal(ks[8], (DEPTH, BRANCH, D_MODEL), F32) * (BRANCH ** -0.5)
    hgrn_lower_bounds = 1.0 + 0.1 * jax.random.normal(ks[9], (N_HGRN, HG_FDIM), F32)
    hgrn_gnorm = 1.0 + 0.1 * jax.random.normal(ks[10], (N_HGRN, HG_DV), F32)
    final_norm_w = 1.0 + 0.1 * jax.random.normal(ks[11], (D_MODEL,), F32)
    return {"x_prompt": x_prompt, "x_sample": x_sample, "cache_k": cache_k, "cache_v": cache_v,
            "state_hgrn": state_hgrn, "page_table": page_table, "norm_w": norm_w, "w_in": w_in,
            "w_out": w_out, "hgrn_lower_bounds": hgrn_lower_bounds, "hgrn_gnorm": hgrn_gnorm,
            "final_norm_w": final_norm_w}


def reference(x_prompt, x_sample, cache_k, cache_v, state_hgrn, page_table, norm_w, w_in, w_out,
              hgrn_lower_bounds, hgrn_gnorm, final_norm_w):
    past_len = page_table.shape[1] * PAGE_SIZE
    B, S = x_prompt.shape[:2]
    Q = x_sample.shape[1]
    pos_p = jnp.arange(S, dtype=F32)
    pos_s = jnp.arange(Q, dtype=F32) + past_len
    sm = jax.nn.softmax(hgrn_lower_bounds.astype(F32), axis=0)
    lbs = jnp.cumsum(sm, axis=0) - sm[0]

    xp, xs = x_prompt, x_sample
    hg_p, hg_s, kp_l, vp_l, ks_l, vs_l = [], [], [], [], [], []
    for i in range(DEPTH):
        hp = rmsnorm(xp, norm_w[i]) @ w_in[i]
        hs = rmsnorm(xs, norm_w[i]) @ w_in[i]
        j = i // N_MIXERS
        if i % N_MIXERS == 0:
            s0 = jnp.zeros((B, HG_HEADS, HG_DK, HG_DV), F32)
            yp, Sp = hgrn_branch(hp, lbs[j], hgrn_gnorm[j], s0)
            ys, Ss = hgrn_branch(hs, lbs[j], hgrn_gnorm[j], state_hgrn[j])
            hg_p.append(Sp)
            hg_s.append(Ss)
        else:
            qp, kp, vp, zp = moba_qkv(hp, pos_p)
            qs, ksm, vsm, zs = moba_qkv(hs, pos_s)
            op = moba_prompt(qp, kp, vp)
            os_ = moba_sample(qs, ksm, vsm, cache_k[j], cache_v[j], page_table)
            yp = op.reshape(B, S, BRANCH) * jax.nn.silu(zp)
            ys = os_.reshape(xs.shape[0], Q, BRANCH) * jax.nn.silu(zs)
            kp_l.append(kp)
            vp_l.append(vp)
            ks_l.append(ksm)
            vs_l.append(vsm)
        xp = xp + (yp @ w_out[i]).astype(xp.dtype)
        xs = xs + (ys @ w_out[i]).astype(xs.dtype)

    y_prompt = rmsnorm(xp, final_norm_w)
    y_sample = rmsnorm(xs, final_norm_w)
    hgrn_prompt = jnp.stack(hg_p)
    k_prompt = jnp.stack(kp_l)
    v_prompt = jnp.stack(vp_l)
    hgrn_sample = jnp.stack(hg_s)
    k_sample = jnp.stack(ks_l)
    v_sample = jnp.stack(vs_l)
    return (y_prompt, y_sample, hgrn_prompt, k_prompt, v_prompt, hgrn_sample, k_sample, v_sample)
```

```python
import functools
import math

import numpy as np
import jax
import jax.numpy as jnp
from jax import lax
from jax.experimental import pallas as pl
from jax.experimental.pallas import tpu as pltpu

F32 = jnp.float32
BF16 = jnp.bfloat16

LANES = 128
SUBLANES = 8
VMEM_LIMIT_BYTES = 56 * 1024 * 1024

N_HEADS = 16
HEAD_DIM = 128
PAGE_SIZE = 128
MOBA_BLOCK = 256
MOBA_TOPK = 3
ROPE_THETA = 10000.0
RMS_EPS = 1e-6
NEG = -1e30
LB_FLOOR = 1e-30

HG_CHUNK = 128
HG_DIAG = 8
HG_LEVELS = (8, 16, 32, 64)


def _cparams(sem):
    return pltpu.CompilerParams(dimension_semantics=sem, vmem_limit_bytes=VMEM_LIMIT_BYTES)


def _silu(x):
    return x / (1.0 + jnp.exp(-x))


def _rms_kernel(x_ref, w_ref, o_ref):
    x = x_ref[...]
    ms = jnp.mean(x * x, axis=-1, keepdims=True)
    o_ref[...] = (x * lax.rsqrt(ms + RMS_EPS) * w_ref[...]).astype(o_ref.dtype)


def _rmsnorm(x, w, tm):
    m, d = x.shape
    return pl.pallas_call(
        _rms_kernel,
        out_shape=jax.ShapeDtypeStruct((m, d), BF16),
        grid=(m // tm,),
        in_specs=[pl.BlockSpec((tm, d), lambda i: (i, 0)),
                  pl.BlockSpec((1, d), lambda i: (0, 0))],
        out_specs=pl.BlockSpec((tm, d), lambda i: (i, 0)),
        compiler_params=_cparams(("parallel",)),
        name="rmsnorm",
    )(x, w.reshape(1, d))


def _epi_plain(acc, extra, outs):
    outs[0][...] = acc


def _epi_silu(acc, extra, outs):
    outs[0][...] = _silu(acc)


def _epi_rope(acc, extra, outs):
    cos_ref, sin_ref = extra
    cos = cos_ref[...]
    sin = sin_ref[...]
    for h in range(N_HEADS):
        xh = acc[:, h * HEAD_DIM:(h + 1) * HEAD_DIM]
        outs[0][:, h * HEAD_DIM:(h + 1) * HEAD_DIM] = (
            xh * cos + pltpu.roll(xh, HEAD_DIM // 2, 1) * sin)


def _hgrn_lower_bound(lbw, layer):
    n = lbw.shape[0]
    rows = [lbw[r:r + 1, :] for r in range(n)]
    mx = rows[0]
    for r in rows[1:]:
        mx = jnp.maximum(mx, r)
    es = [jnp.exp(r - mx) for r in rows]
    tot = es[0]
    for e in es[1:]:
        tot = tot + e
    sm = [e / tot for e in es]
    cs = sm[0]
    for r in range(1, layer + 1):
        cs = cs + sm[r]
    return cs - sm[0]


def _epi_gates(acc, extra, outs, *, layer):
    lb = _hgrn_lower_bound(extra[0][...], layer)
    log_lb = jnp.log(jnp.maximum(lb, LB_FLOOR))
    l1m = jnp.log1p(-lb)
    z = acc
    log_sig = jnp.minimum(z, 0.0) - jnp.log1p(jnp.exp(-jnp.abs(z)))
    b = l1m + log_sig
    hi = jnp.maximum(log_lb, b)
    lo = jnp.minimum(log_lb, b)
    outs[0][...] = hi + jnp.log1p(jnp.exp(lo - hi))
    outs[1][...] = (1.0 - lb) / (1.0 + jnp.exp(z))


def _proj_kernel(*refs, epilogue, n_extra):
    x_ref, w_ref = refs[0], refs[1]
    extra = refs[2:2 + n_extra]
    outs = refs[2 + n_extra:]
    acc = jnp.dot(x_ref[...], w_ref[...], preferred_element_type=F32)
    epilogue(acc, extra, outs)


def _proj(xn, w, layer, group, epilogue, extra, extra_specs, n_out, tm, name):
    m, d = xn.shape
    kern = functools.partial(_proj_kernel, epilogue=epilogue, n_extra=len(extra))
    out = pl.pallas_call(
        kern,
        out_shape=[jax.ShapeDtypeStruct((m, d), F32)] * n_out,
        grid=(m // tm,),
        in_specs=[pl.BlockSpec((tm, d), lambda i: (i, 0)),
                  pl.BlockSpec((None, d, d), lambda i: (layer, 0, group))] + list(extra_specs),
        out_specs=[pl.BlockSpec((tm, d), lambda i: (i, 0))] * n_out,
        compiler_params=_cparams(("parallel",)),
        name=name,
    )(xn, w, *extra)
    return out


def _rope_tables(pos):
    half = HEAD_DIM // 2
    inv = jnp.power(ROPE_THETA, -jnp.arange(half, dtype=F32) / half)
    ang = pos[:, None] * inv[None, :]
    cos = jnp.cos(ang)
    sin = jnp.sin(ang)
    return jnp.concatenate([cos, cos], axis=-1), jnp.concatenate([-sin, sin], axis=-1)


def _proj_hgrn(xn, w, layer, lbw, hg_layer, tm):
    (q,) = _proj(xn, w, layer, 0, _epi_silu, (), (), 1, tm, "proj_hgrn_q")
    lf, k = _proj(xn, w, layer, 1, functools.partial(_epi_gates, layer=hg_layer), (lbw,),
                  (pl.BlockSpec(lbw.shape, lambda i: (0, 0)),), 2, tm, "proj_hgrn_f")
    (v,) = _proj(xn, w, layer, 2, _epi_plain, (), (), 1, tm, "proj_hgrn_i")
    (sz,) = _proj(xn, w, layer, 3, _epi_silu, (), (), 1, tm, "proj_z")
    return q, lf, k, v, sz


def _proj_moba(xn, w, layer, cos, sin, tm):
    nblk = cos.shape[0] // tm
    tspec = pl.BlockSpec((tm, HEAD_DIM), lambda i: (i % nblk, 0))
    (q,) = _proj(xn, w, layer, 0, _epi_rope, (cos, sin), (tspec, tspec), 1, tm, "proj_moba_q")
    (k,) = _proj(xn, w, layer, 1, _epi_rope, (cos, sin), (tspec, tspec), 1, tm, "proj_moba_k")
    (v,) = _proj(xn, w, layer, 2, _epi_plain, (), (), 1, tm, "proj_moba_v")
    (sz,) = _proj(xn, w, layer, 3, _epi_silu, (), (), 1, tm, "proj_z")
    return q, k, v, sz


def _oproj_kernel(y_ref, w_ref, x_ref, nw_ref, *outs, emit_x):
    xo = x_ref[...] + jnp.dot(y_ref[...].astype(BF16), w_ref[...], preferred_element_type=F32)
    ms = jnp.mean(xo * xo, axis=-1, keepdims=True)
    xn = xo * lax.rsqrt(ms + RMS_EPS) * nw_ref[...]
    if emit_x:
        outs[0][...] = xo
        outs[1][...] = xn.astype(outs[1].dtype)
    else:
        outs[0][...] = xn.astype(outs[0].dtype)


def _oproj(y, w, layer, x, nw, tm, last):
    m, d = x.shape
    row = pl.BlockSpec((tm, d), lambda i: (i, 0))
    if last:
        out_shape = [jax.ShapeDtypeStruct((m, d), F32)]
    else:
        out_shape = [jax.ShapeDtypeStruct((m, d), F32), jax.ShapeDtypeStruct((m, d), BF16)]
    return pl.pallas_call(
        functools.partial(_oproj_kernel, emit_x=not last),
        out_shape=out_shape,
        grid=(m // tm,),
        in_specs=[row, pl.BlockSpec((None, d, d), lambda i: (layer, 0, 0)), row,
                  pl.BlockSpec((1, d), lambda i: (0, 0))],
        out_specs=[row] * len(out_shape),
        compiler_params=_cparams(("parallel",)),
        name="oproj",
    )(y, w, x, nw.reshape(1, d))


def _hgrn_level_masks():
    c = HG_CHUNK
    t = np.arange(c)[:, None]
    s = np.arange(c)[None, :]
    masks = []
    for m in HG_LEVELS:
        masks.append(((t // (2 * m) == s // (2 * m)) & (t % (2 * m) >= m) & (s % (2 * m) < m)))
    return np.stack(masks).astype(np.float32)


def _split3(x):
    hi = x.astype(BF16)
    r1 = x - hi.astype(F32)
    mid = r1.astype(BF16)
    lo = (r1 - mid.astype(F32)).astype(BF16)
    return hi, mid, lo


def _hgrn_scan_kernel(q_ref, lf_ref, k_ref, v_ref, sz_ref, g_ref, tri_ref, mask_ref,
                      y_ref, s_ref, st_ref):
    c = HG_CHUNK
    n_chunks = q_ref.shape[0] // c
    st_ref[...] = jnp.zeros_like(st_ref)
    lane = lax.broadcasted_iota(jnp.int32, (HG_DIAG, c), 1)
    row = lax.broadcasted_iota(jnp.int32, (HG_DIAG, c), 0)

    def chunk(ci, carry):
        r0 = pl.multiple_of(ci * c, c)
        q = q_ref[pl.ds(r0, c), :]
        lf = lf_ref[pl.ds(r0, c), :]
        k = k_ref[pl.ds(r0, c), :]
        v = v_ref[pl.ds(r0, c), :]
        tri = tri_ref[...]
        hi, mid, lo = _split3(lf)
        b = (jnp.dot(tri, hi, preferred_element_type=F32)
             + jnp.dot(tri, mid, preferred_element_type=F32)
             + jnp.dot(tri, lo, preferred_element_type=F32))
        a = jnp.zeros((c, c), F32)
        for li, m in enumerate(HG_LEVELS):
            pieces = []
            for blk in range(c // (2 * m)):
                bnd = blk * 2 * m + m - 1
                pieces.append(jnp.broadcast_to(b[bnd:bnd + 1, :], (2 * m, HEAD_DIM)))
            ref_b = pieces[0] if len(pieces) == 1 else jnp.concatenate(pieces, axis=0)
            qs = (q * jnp.exp(jnp.minimum(b - ref_b, 0.0))).astype(BF16)
            ks = (k * jnp.exp(jnp.minimum(ref_b - b, 0.0))).astype(BF16)
            am = lax.dot_general(qs, ks, (((1,), (1,)), ((), ())), preferred_element_type=F32)
            a = a + am * mask_ref[li]
        rows = []
        for g in range(c // HG_DIAG):
            g0 = g * HG_DIAG
            qg = q[g0:g0 + HG_DIAG, :]
            bg = b[g0:g0 + HG_DIAG, :]
            kg = k[g0:g0 + HG_DIAG, :]
            ag = jnp.zeros((HG_DIAG, c), F32)
            for s in range(HG_DIAG):
                w = qg * jnp.exp(jnp.minimum(bg - bg[s:s + 1, :], 0.0)) * kg[s:s + 1, :]
                col = jnp.sum(w, axis=-1, keepdims=True)
                ag = jnp.where((lane == g0 + s) & (row >= s), col, ag)
            rows.append(ag)
        a = a + jnp.concatenate(rows, axis=0)
        st = st_ref[...]
        qd = (q * jnp.exp(b)).astype(BF16)
        o = (jnp.dot(a.astype(BF16), v.astype(BF16), preferred_element_type=F32)
             + lax.dot_general(qd, st.astype(BF16), (((1,), (1,)), ((), ())),
                               preferred_element_type=F32))
        bl = b[c - 1:c, :]
        kd = (k * jnp.exp(bl - b)).astype(BF16)
        st_ref[...] = st * jnp.exp(bl) + jnp.dot(v.T.astype(BF16), kd,
                                                  preferred_element_type=F32)
        ms = jnp.mean(o * o, axis=-1, keepdims=True)
        y = o * lax.rsqrt(ms + RMS_EPS) * g_ref[...] * sz_ref[pl.ds(r0, c), :]
        y_ref[pl.ds(r0, c), :] = y.astype(y_ref.dtype)
        return carry

    lax.fori_loop(0, n_chunks, chunk, 0)
    s_ref[...] = st_ref[...].T


def _hgrn_scan(q, lf, k, v, sz, gnorm, batch, seq):
    m, d = q.shape
    tri = jnp.asarray(np.tril(np.ones((HG_CHUNK, HG_CHUNK), np.float32)), BF16)
    masks = jnp.asarray(_hgrn_level_masks())
    col = pl.BlockSpec((seq, HEAD_DIM), lambda b, h: (b, h))
    y, s = pl.pallas_call(
        _hgrn_scan_kernel,
        out_shape=[jax.ShapeDtypeStruct((m, d), BF16),
                   jax.ShapeDtypeStruct((batch, N_HEADS, HEAD_DIM, HEAD_DIM), F32)],
        grid=(batch, N_HEADS),
        in_specs=[col, col, col, col, col,
                  pl.BlockSpec((1, HEAD_DIM), lambda b, h: (0, 0)),
                  pl.BlockSpec(tri.shape, lambda b, h: (0, 0)),
                  pl.BlockSpec(masks.shape, lambda b, h: (0, 0, 0))],
        out_specs=[col, pl.BlockSpec((None, None, HEAD_DIM, HEAD_DIM), lambda b, h: (b, h, 0, 0))],
        scratch_shapes=[pltpu.VMEM((HEAD_DIM, HEAD_DIM), F32)],
        compiler_params=_cparams(("parallel", "parallel")),
        name="hgrn_scan",
    )(q, lf, k, v, sz, gnorm.reshape(1, HEAD_DIM), tri, masks)
    return y, s


def _lane_bcast_col(rowvec):
    return jnp.broadcast_to(rowvec, (HEAD_DIM, HEAD_DIM)).T


def _hgrn_step_kernel(q_ref, lf_ref, k_ref, v_ref, sz_ref, g_ref, s0_ref, y_ref, s_ref):
    b = pl.program_id(1)
    sl = pl.ds(b, 1)
    f_col = _lane_bcast_col(jnp.exp(lf_ref[sl, :]))
    k_col = _lane_bcast_col(k_ref[sl, :])
    q_col = _lane_bcast_col(q_ref[sl, :])
    s_new = f_col * s0_ref[...] + k_col * v_ref[sl, :]
    s_ref[...] = s_new
    o = jnp.sum(q_col * s_new, axis=0, keepdims=True)
    ms = jnp.mean(o * o, axis=-1, keepdims=True)
    y = o * lax.rsqrt(ms + RMS_EPS) * g_ref[...] * sz_ref[sl, :]
    y_ref[sl, :] = y.astype(y_ref.dtype)


def _hgrn_step(q, lf, k, v, sz, gnorm, state_all, layer):
    db, d = q.shape
    col = pl.BlockSpec((db, HEAD_DIM), lambda h, b: (0, h))
    y, s = pl.pallas_call(
        _hgrn_step_kernel,
        out_shape=[jax.ShapeDtypeStruct((db, d), F32),
                   jax.ShapeDtypeStruct((db, N_HEADS, HEAD_DIM, HEAD_DIM), F32)],
        grid=(N_HEADS, db),
        in_specs=[col, col, col, col, col,
                  pl.BlockSpec((1, HEAD_DIM), lambda h, b: (0, 0)),
                  pl.BlockSpec((None, None, None, HEAD_DIM, HEAD_DIM),
                               lambda h, b: (layer, b, h, 0, 0))],
        out_specs=[col, pl.BlockSpec((None, None, HEAD_DIM, HEAD_DIM), lambda h, b: (b, h, 0, 0))],
        compiler_params=_cparams(("parallel", "arbitrary")),
        name="hgrn_step",
    )(q, lf, k, v, sz, gnorm.reshape(1, HEAD_DIM), state_all)
    return y, s


def _moba_prompt_kernel(q_ref, k_ref, v_ref, sz_ref, y_ref, means_ref, vt_ref, sel_ref,
                        m_ref, l_ref, acc_ref):
    blk = MOBA_BLOCK
    nb = k_ref.shape[0] // blk
    i = pl.program_id(2)

    @pl.when(i == 0)
    def _():
        for n in range(nb):
            kb = k_ref[n * blk:(n + 1) * blk, :]
            means_ref[n:n + 1, :] = jnp.sum(kb, axis=0, keepdims=True) * (1.0 / blk)
            vt_ref[n] = v_ref[n * blk:(n + 1) * blk, :].T.astype(BF16)

    q = q_ref[...]
    st = lax.dot_general(means_ref[...], q, (((1,), (1,)), ((), ())),
                         precision=lax.Precision.HIGHEST, preferred_element_type=F32)
    nidx = lax.broadcasted_iota(jnp.int32, (nb, blk), 0)
    past = nidx < i
    sm = jnp.where(past, st, NEG)
    rank = jnp.zeros((nb, blk), F32)
    for m in range(nb):
        rm = sm[m:m + 1, :]
        beats = (rm > sm) | ((rm == sm) & (m < nidx))
        rank = rank + jnp.where(beats, 1.0, 0.0)
    sel_ref[...] = jnp.where(past & (rank < float(MOBA_TOPK)), 1.0, 0.0)

    m_ref[...] = jnp.full_like(m_ref, NEG)
    l_ref[...] = jnp.zeros_like(l_ref)
    acc_ref[...] = jnp.zeros_like(acc_ref)
    qb = q.astype(BF16)
    lead = (lax.broadcasted_iota(jnp.int32, (blk, blk), 0)
            - lax.broadcasted_iota(jnp.int32, (blk, blk), 1))
    scale = HEAD_DIM ** -0.5

    def kv_block(n, carry):
        k0 = pl.multiple_of(n * blk, blk)
        kb = k_ref[pl.ds(k0, blk), :].astype(BF16)
        s = lax.dot_general(kb, qb, (((1,), (1,)), ((), ())),
                            preferred_element_type=F32) * scale
        own_limit = jnp.where(n == i, 0, -2 * blk)
        picked = jnp.broadcast_to(sel_ref[pl.ds(n, 1), :], (blk, blk))
        mask = jnp.where(lead <= own_limit, 1.0, picked) > 0.0
        s = jnp.where(mask, s, NEG)
        m_old = m_ref[...]
        m_new = jnp.maximum(m_old, jnp.max(s, axis=0, keepdims=True))
        alpha = jnp.exp(m_old - m_new)
        p = jnp.where(mask, jnp.exp(s - m_new), 0.0)
        l_ref[...] = alpha * l_ref[...] + jnp.sum(p, axis=0, keepdims=True)
        acc_ref[...] = alpha * acc_ref[...] + jnp.dot(
            vt_ref[n], p.astype(BF16), preferred_element_type=F32)
        m_ref[...] = m_new
        return carry

    lax.fori_loop(0, i + 1, kv_block, 0)
    o = (acc_ref[...] / l_ref[...]).T
    y_ref[...] = (o * sz_ref[...]).astype(y_ref.dtype)


def _moba_prompt(q, k, v, sz, batch, seq):
    m, d = q.shape
    nq = seq // MOBA_BLOCK
    nb = seq // MOBA_BLOCK
    qspec = pl.BlockSpec((MOBA_BLOCK, HEAD_DIM), lambda b, h, i: (b * nq + i, h))
    kvspec = pl.BlockSpec((seq, HEAD_DIM), lambda b, h, i: (b, h))
    return pl.pallas_call(
        _moba_prompt_kernel,
        out_shape=jax.ShapeDtypeStruct((m, d), BF16),
        grid=(batch, N_HEADS, nq),
        in_specs=[qspec, kvspec, kvspec, qspec],
        out_specs=qspec,
        scratch_shapes=[pltpu.VMEM((nb, HEAD_DIM), F32),
                        pltpu.VMEM((nb, HEAD_DIM, MOBA_BLOCK), BF16),
                        pltpu.VMEM((nb, MOBA_BLOCK), F32),
                        pltpu.VMEM((1, MOBA_BLOCK), F32),
                        pltpu.VMEM((1, MOBA_BLOCK), F32),
                        pltpu.VMEM((HEAD_DIM, MOBA_BLOCK), F32)],
        compiler_params=_cparams(("parallel", "parallel", "arbitrary")),
        name="moba_prompt",
    )(q, k, v, sz)


def _page_sum_kernel(pt_ref, page_ref, o_ref):
    p = pl.program_id(1)
    s = jnp.sum(page_ref[...], axis=0)

    @pl.when(p % 2 == 0)
    def _():
        o_ref[...] = s

    @pl.when(p % 2 == 1)
    def _():
        o_ref[...] = o_ref[...] + s


def _block_sums(cache, page_table, layer):
    db, n_pages = page_table.shape
    ppb = MOBA_BLOCK // PAGE_SIZE
    assert ppb == 2
    grid_spec = pltpu.PrefetchScalarGridSpec(
        num_scalar_prefetch=1,
        grid=(db, n_pages),
        in_specs=[pl.BlockSpec((None, None, PAGE_SIZE, N_HEADS, HEAD_DIM),
                               lambda b, p, pt: (layer, pt[b * n_pages + p], 0, 0, 0))],
        out_specs=pl.BlockSpec((None, None, N_HEADS, HEAD_DIM),
                               lambda b, p, pt: (b, p // ppb, 0, 0)),
    )
    return pl.pallas_call(
        _page_sum_kernel,
        out_shape=jax.ShapeDtypeStruct((db, n_pages // ppb, N_HEADS, HEAD_DIM), F32),
        grid_spec=grid_spec,
        compiler_params=_cparams(("parallel", "arbitrary")),
        name="page_sums",
    )(page_table.reshape(-1), cache)


def _select_kernel(q_ref, sums_ref, idx_ref):
    nblk = sums_ref.shape[0]
    lane = lax.broadcasted_iota(jnp.int32, (nblk, LANES), 1)
    sc = jnp.full((nblk, LANES), NEG, F32)
    for h in range(N_HEADS):
        prod = sums_ref[:, h, :] * (1.0 / MOBA_BLOCK) * q_ref[h:h + 1, :]
        col = jnp.sum(prod, axis=-1, keepdims=True)
        sc = jnp.where(lane == h, col, sc)
    nidx = lax.broadcasted_iota(jnp.int32, (nblk, LANES), 0).astype(F32)
    out = jnp.zeros((SUBLANES, LANES), F32)
    orow = lax.broadcasted_iota(jnp.int32, (SUBLANES, LANES), 0)
    for t in range(MOBA_TOPK):
        mx = jnp.max(sc, axis=0, keepdims=True)
        pick = jnp.min(jnp.where(sc == mx, nidx, float(nblk)), axis=0, keepdims=True)
        out = jnp.where(orow == t, pick, out)
        sc = jnp.where(nidx == pick, -jnp.inf, sc)
    idx_ref[...] = out.astype(jnp.int32)


def _select_blocks(q, sums):
    db, d = q.shape
    nblk = sums.shape[1]
    idx = pl.pallas_call(
        _select_kernel,
        out_shape=jax.ShapeDtypeStruct((db, SUBLANES, LANES), jnp.int32),
        grid=(db,),
        in_specs=[pl.BlockSpec((None, N_HEADS, HEAD_DIM), lambda b: (b, 0, 0)),
                  pl.BlockSpec((None, nblk, N_HEADS, HEAD_DIM), lambda b: (b, 0, 0, 0))],
        out_specs=pl.BlockSpec((None, SUBLANES, LANES), lambda b: (b, 0, 0)),
        compiler_params=_cparams(("parallel",)),
        name="moba_select",
    )(q.reshape(db, N_HEADS, HEAD_DIM), sums)
    return jnp.transpose(idx[:, :MOBA_TOPK, :N_HEADS], (0, 2, 1))


def _moba_decode_kernel(idx_ref, pt_ref, q_ref, kn_ref, vn_ref, sz_ref, kp_ref, vp_ref,
                        y_ref, m_ref, l_ref, acc_ref):
    h = pl.program_id(0)
    b = pl.program_id(1)
    t = pl.program_id(2)
    row = pl.ds(b * N_HEADS + h, 1)
    scale = HEAD_DIM ** -0.5
    q = q_ref[row, :]
    mine = lax.broadcasted_iota(jnp.int32, (SUBLANES, HEAD_DIM), 0) == h % SUBLANES

    @pl.when(t == 0)
    def _():
        s_self = jnp.sum(q * kn_ref[row, :], axis=-1, keepdims=True) * scale
        m_ref[...] = s_self[None]
        l_ref[...] = jnp.ones_like(l_ref)
        acc_ref[...] = jnp.where(mine, vn_ref[row, :], 0.0)

    qm = jnp.where(mine, q, 0.0)
    prod = kp_ref[...] * qm[None]
    s = jnp.sum(jnp.sum(prod, axis=2, keepdims=True), axis=1, keepdims=True) * scale
    m_old = m_ref[...]
    m_new = jnp.maximum(m_old, jnp.max(s, axis=0, keepdims=True))
    alpha = jnp.exp(m_old - m_new)
    p = jnp.exp(s - m_new)
    l_ref[...] = alpha * l_ref[...] + jnp.sum(p, axis=0, keepdims=True)
    acc_ref[...] = alpha[0] * acc_ref[...] + jnp.sum(p * vp_ref[...], axis=0)
    m_ref[...] = m_new

    @pl.when(t == pl.num_programs(2) - 1)
    def _():
        o = jnp.sum(jnp.where(mine, acc_ref[...], 0.0), axis=0, keepdims=True)
        y_ref[row, :] = o / l_ref[...][0] * sz_ref[row, :]


def _moba_decode(q, k_new, v_new, sz, cache_k, cache_v, idx, page_table, layer):
    db, d = q.shape
    n_pages = page_table.shape[1]
    ppb = MOBA_BLOCK // PAGE_SIZE
    steps = MOBA_TOPK * ppb

    def page_map(h, b, t, idx_s, pt_s):
        blk = idx_s[(b * N_HEADS + h) * MOBA_TOPK + t // ppb]
        return (layer, pt_s[b * n_pages + blk * ppb + t % ppb], 0, h // SUBLANES, 0)

    rows = db * N_HEADS
    whole = pl.BlockSpec((rows, HEAD_DIM), lambda h, b, t, idx_s, pt_s: (0, 0))
    pspec = pl.BlockSpec((None, None, PAGE_SIZE, SUBLANES, HEAD_DIM), page_map)
    grid_spec = pltpu.PrefetchScalarGridSpec(
        num_scalar_prefetch=2,
        grid=(N_HEADS, db, steps),
        in_specs=[whole, whole, whole, whole, pspec, pspec],
        out_specs=whole,
        scratch_shapes=[pltpu.VMEM((1, 1, 1), F32), pltpu.VMEM((1, 1, 1), F32),
                        pltpu.VMEM((SUBLANES, HEAD_DIM), F32)],
    )
    as_rows = lambda a: a.reshape(rows, HEAD_DIM)
    y = pl.pallas_call(
        _moba_decode_kernel,
        out_shape=jax.ShapeDtypeStruct((rows, HEAD_DIM), F32),
        grid_spec=grid_spec,
        compiler_params=_cparams(("arbitrary", "arbitrary", "arbitrary")),
        name="moba_decode",
    )(idx.reshape(-1), page_table.reshape(-1), as_rows(q), as_rows(k_new), as_rows(v_new),
      as_rows(sz), cache_k, cache_v)
    return y.reshape(db, d)


def _row_tile(m):
    return 256 if m % 256 == 0 else m


def kernel(x_prompt, x_sample, cache_k, cache_v, state_hgrn, page_table, norm_w, w_in, w_out,
           hgrn_lower_bounds, hgrn_gnorm, final_norm_w):
    batch, seq, d = x_prompt.shape
    db, dq, _ = x_sample.shape
    depth = norm_w.shape[0]
    past_len = page_table.shape[1] * PAGE_SIZE
    assert dq == 1 and d == N_HEADS * HEAD_DIM and seq % MOBA_BLOCK == 0 and seq % HG_CHUNK == 0
    assert past_len % MOBA_BLOCK == 0 and past_len // MOBA_BLOCK >= MOBA_TOPK

    w_in_b = w_in.astype(BF16)
    w_out_b = w_out.astype(BF16)
    cos_p, sin_p = _rope_tables(jnp.arange(seq, dtype=F32))
    cos_s, sin_s = _rope_tables(jnp.arange(dq, dtype=F32) + past_len)
    cos_s = jnp.broadcast_to(cos_s, (db * dq, HEAD_DIM))
    sin_s = jnp.broadcast_to(sin_s, (db * dq, HEAD_DIM))

    xp = x_prompt.reshape(batch * seq, d)
    xs = x_sample.reshape(db * dq, d)
    tp = _row_tile(xp.shape[0])
    ts = _row_tile(xs.shape[0])
    xnp = _rmsnorm(xp, norm_w[0], tp)
    xns = _rmsnorm(xs, norm_w[0], ts)

    hg_p, hg_s, kp_l, vp_l, ks_l, vs_l = [], [], [], [], [], []
    for i in range(depth):
        j = i // 2
        if i % 2 == 0:
            q, lf, k, v, sz = _proj_hgrn(xnp, w_in_b, i, hgrn_lower_bounds, j, tp)
            yp, sp = _hgrn_scan(q, lf, k, v, sz, hgrn_gnorm[j], batch, seq)
            q, lf, k, v, sz = _proj_hgrn(xns, w_in_b, i, hgrn_lower_bounds, j, ts)
            ys, ss = _hgrn_step(q, lf, k, v, sz, hgrn_gnorm[j], state_hgrn, j)
            hg_p.append(sp)
            hg_s.append(ss)
        else:
            q, k, v, sz = _proj_moba(xnp, w_in_b, i, cos_p, sin_p, tp)
            yp = _moba_prompt(q, k, v, sz, batch, seq)
            kp_l.append(k)
            vp_l.append(v)
            q, k, v, sz = _proj_moba(xns, w_in_b, i, cos_s, sin_s, ts)
            sums = _block_sums(cache_k, page_table, j)
            idx = _select_blocks(q, sums)
            ys = _moba_decode(q, k, v, sz, cache_k, cache_v, idx, page_table, j)
            ks_l.append(k)
            vs_l.append(v)
        last = i == depth - 1
        nw = final_norm_w if last else norm_w[i + 1]
        outp = _oproj(yp, w_out_b, i, xp, nw, tp, last)
        outs = _oproj(ys, w_out_b, i, xs, nw, ts, last)
        if last:
            y_prompt, y_sample = outp[0], outs[0]
        else:
            xp, xnp = outp
            xs, xns = outs

    shp = (batch, seq, N_HEADS, HEAD_DIM)
    shs = (db, dq, N_HEADS, HEAD_DIM)
    return (y_prompt.reshape(batch, seq, d),
            y_sample.reshape(db, dq, d),
            jnp.stack(hg_p),
            jnp.stack([t.reshape(shp) for t in kp_l]),
            jnp.stack([t.reshape(shp) for t in vp_l]),
            jnp.stack(hg_s),
            jnp.stack([t.reshape(shs) for t in ks_l]),
            jnp.stack([t.reshape(shs) for t in vs_l]))
```

```python
import functools
import math

import numpy as np
import jax
import jax.numpy as jnp
from jax import lax
from jax.experimental import pallas as pl
from jax.experimental.pallas import tpu as pltpu

F32 = jnp.float32
BF16 = jnp.bfloat16

LANES = 128
SUBLANES = 8
VMEM_LIMIT_BYTES = 56 * 1024 * 1024

N_HEADS = 16
HEAD_DIM = 128
PAGE_SIZE = 128
MOBA_BLOCK = 256
MOBA_TOPK = 3
ROPE_THETA = 10000.0
RMS_EPS = 1e-6
NEG = -1e30
LB_FLOOR = 1e-30

HG_CHUNK = 128
HG_DIAG = 8
HG_LEVELS = (8, 16, 32, 64)
HG_HEADS_PER_STEP = 2
MOBA_HEADS_PER_STEP = 4
PAGES_PER_STEP = 16


def _cparams(sem):
    return pltpu.CompilerParams(dimension_semantics=sem, vmem_limit_bytes=VMEM_LIMIT_BYTES)


def _silu(x):
    return x / (1.0 + jnp.exp(-x))


def _rms_kernel(x_ref, w_ref, o_ref):
    x = x_ref[...]
    ms = jnp.mean(x * x, axis=-1, keepdims=True)
    o_ref[...] = (x * lax.rsqrt(ms + RMS_EPS) * w_ref[...]).astype(o_ref.dtype)


def _rmsnorm(x, w, tm):
    m, d = x.shape
    return pl.pallas_call(
        _rms_kernel,
        out_shape=jax.ShapeDtypeStruct((m, d), BF16),
        grid=(m // tm,),
        in_specs=[pl.BlockSpec((tm, d), lambda i: (i, 0)),
                  pl.BlockSpec((1, d), lambda i: (0, 0))],
        out_specs=pl.BlockSpec((tm, d), lambda i: (i, 0)),
        compiler_params=_cparams(("parallel",)),
        name="rmsnorm",
    )(x, w.reshape(1, d))


def _epi_plain(acc, extra, outs):
    outs[0][...] = acc


def _epi_silu(acc, extra, outs):
    outs[0][...] = _silu(acc)


def _epi_rope(acc, extra, outs):
    cos_ref, sin_ref = extra
    cos = cos_ref[...]
    sin = sin_ref[...]
    for h in range(N_HEADS):
        xh = acc[:, h * HEAD_DIM:(h + 1) * HEAD_DIM]
        outs[0][:, h * HEAD_DIM:(h + 1) * HEAD_DIM] = (
            xh * cos + pltpu.roll(xh, HEAD_DIM // 2, 1) * sin)


def _hgrn_lower_bound(lbw, layer):
    n = lbw.shape[0]
    rows = [lbw[r:r + 1, :] for r in range(n)]
    mx = rows[0]
    for r in rows[1:]:
        mx = jnp.maximum(mx, r)
    es = [jnp.exp(r - mx) for r in rows]
    tot = es[0]
    for e in es[1:]:
        tot = tot + e
    sm = [e / tot for e in es]
    cs = sm[0]
    for r in range(1, layer + 1):
        cs = cs + sm[r]
    return cs - sm[0]


def _epi_gates(acc, extra, outs, *, layer):
    lb = _hgrn_lower_bound(extra[0][...], layer)
    lb_floor = jnp.maximum(lb, LB_FLOOR)
    one_m = 1.0 - lb
    z = acc
    t = jnp.exp(-jnp.abs(z))
    r = 1.0 / (1.0 + t)
    pos = z >= 0.0
    sig = jnp.where(pos, r, t * r)
    nsig = jnp.where(pos, t * r, r)
    outs[0][...] = jnp.log2(lb_floor + one_m * sig)
    outs[1][...] = one_m * nsig


def _proj_kernel(*refs, epilogue, n_extra):
    x_ref, w_ref = refs[0], refs[1]
    extra = refs[2:2 + n_extra]
    outs = refs[2 + n_extra:]
    acc = jnp.dot(x_ref[...], w_ref[...], preferred_element_type=F32)
    epilogue(acc, extra, outs)


def _proj(xn, w, layer, group, epilogue, extra, extra_specs, n_out, tm, name):
    m, d = xn.shape
    kern = functools.partial(_proj_kernel, epilogue=epilogue, n_extra=len(extra))
    out = pl.pallas_call(
        kern,
        out_shape=[jax.ShapeDtypeStruct((m, d), F32)] * n_out,
        grid=(m // tm,),
        in_specs=[pl.BlockSpec((tm, d), lambda i: (i, 0)),
                  pl.BlockSpec((None, d, d), lambda i: (layer, 0, group))] + list(extra_specs),
        out_specs=[pl.BlockSpec((tm, d), lambda i: (i, 0))] * n_out,
        compiler_params=_cparams(("parallel",)),
        name=name,
    )(xn, w, *extra)
    return out


def _rope_tables(pos):
    half = HEAD_DIM // 2
    inv = jnp.power(ROPE_THETA, -jnp.arange(half, dtype=F32) / half)
    ang = pos[:, None] * inv[None, :]
    cos = jnp.cos(ang)
    sin = jnp.sin(ang)
    return jnp.concatenate([cos, cos], axis=-1), jnp.concatenate([-sin, sin], axis=-1)


def _proj_hgrn(xn, w, layer, lbw, hg_layer, tm):
    (q,) = _proj(xn, w, layer, 0, _epi_silu, (), (), 1, tm, "proj_hgrn_q")
    lf, k = _proj(xn, w, layer, 1, functools.partial(_epi_gates, layer=hg_layer), (lbw,),
                  (pl.BlockSpec(lbw.shape, lambda i: (0, 0)),), 2, tm, "proj_hgrn_f")
    (v,) = _proj(xn, w, layer, 2, _epi_plain, (), (), 1, tm, "proj_hgrn_i")
    (sz,) = _proj(xn, w, layer, 3, _epi_silu, (), (), 1, tm, "proj_z")
    return q, lf, k, v, sz


def _proj_moba(xn, w, layer, cos, sin, tm):
    nblk = cos.shape[0] // tm
    tspec = pl.BlockSpec((tm, HEAD_DIM), lambda i: (i % nblk, 0))
    (q,) = _proj(xn, w, layer, 0, _epi_rope, (cos, sin), (tspec, tspec), 1, tm, "proj_moba_q")
    (k,) = _proj(xn, w, layer, 1, _epi_rope, (cos, sin), (tspec, tspec), 1, tm, "proj_moba_k")
    (v,) = _proj(xn, w, layer, 2, _epi_plain, (), (), 1, tm, "proj_moba_v")
    (sz,) = _proj(xn, w, layer, 3, _epi_silu, (), (), 1, tm, "proj_z")
    return q, k, v, sz


def _oproj_kernel(y_ref, w_ref, x_ref, nw_ref, *outs, emit_x):
    xo = x_ref[...] + jnp.dot(y_ref[...].astype(BF16), w_ref[...], preferred_element_type=F32)
    ms = jnp.mean(xo * xo, axis=-1, keepdims=True)
    xn = xo * lax.rsqrt(ms + RMS_EPS) * nw_ref[...]
    if emit_x:
        outs[0][...] = xo
        outs[1][...] = xn.astype(outs[1].dtype)
    else:
        outs[0][...] = xn.astype(outs[0].dtype)


def _oproj(y, w, layer, x, nw, tm, last):
    m, d = x.shape
    row = pl.BlockSpec((tm, d), lambda i: (i, 0))
    if last:
        out_shape = [jax.ShapeDtypeStruct((m, d), F32)]
    else:
        out_shape = [jax.ShapeDtypeStruct((m, d), F32), jax.ShapeDtypeStruct((m, d), BF16)]
    return pl.pallas_call(
        functools.partial(_oproj_kernel, emit_x=not last),
        out_shape=out_shape,
        grid=(m // tm,),
        in_specs=[row, pl.BlockSpec((None, d, d), lambda i: (layer, 0, 0)), row,
                  pl.BlockSpec((1, d), lambda i: (0, 0))],
        out_specs=[row] * len(out_shape),
        compiler_params=_cparams(("parallel",)),
        name="oproj",
    )(y, w, x, nw.reshape(1, d))


def _hgrn_level_masks():
    c = HG_CHUNK
    t = np.arange(c)[:, None]
    s = np.arange(c)[None, :]
    masks = []
    for m in HG_LEVELS:
        masks.append(((t // (2 * m) == s // (2 * m)) & (t % (2 * m) >= m) & (s % (2 * m) < m)))
    return np.stack(masks).astype(np.float32)


def _split3(x):
    hi = x.astype(BF16)
    r1 = x - hi.astype(F32)
    mid = r1.astype(BF16)
    lo = (r1 - mid.astype(F32)).astype(BF16)
    return hi, mid, lo


def _hgrn_chunks(qs, lfs, ks, vs, sts, tri3, mask_ref, place_ref, bk_ref):
    c = HG_CHUNK
    heads = range(len(qs))
    nt = (((1,), (1,)), ((), ()))
    bs = [jnp.dot(tri3, jnp.concatenate(_split3(lfs[h]), axis=0), preferred_element_type=F32)
          for h in heads]
    accs = [jnp.zeros((c, c), F32) for _ in heads]
    for li, m in enumerate(HG_LEVELS):
        ams = []
        for h in heads:
            b = bs[h]
            pieces = []
            for blk in range(c // (2 * m)):
                bnd = blk * 2 * m + m - 1
                pieces.append(jnp.broadcast_to(b[bnd:bnd + 1, :], (2 * m, HEAD_DIM)))
            ref_b = pieces[0] if len(pieces) == 1 else jnp.concatenate(pieces, axis=0)
            d = b - ref_b
            e = jnp.exp2(jnp.minimum(d, -d))
            ams.append(lax.dot_general((qs[h] * e).astype(BF16), (ks[h] * e).astype(BF16), nt,
                                       preferred_element_type=F32))
        for h in heads:
            accs[h] = accs[h] + ams[h] * mask_ref[li]
    for h in heads:
        b = bs[h]
        bk_ref[h, 0] = b
        bk_ref[h, 1] = ks[h]
        rows = []
        for g in range(c // HG_DIAG):
            g0 = g * HG_DIAG
            qg = qs[h][g0:g0 + HG_DIAG, :]
            bg = b[g0:g0 + HG_DIAG, :]
            ag = jnp.zeros((HG_DIAG, c), F32)
            for s in range(HG_DIAG):
                b_s = bk_ref[h, 0, g0 + s:g0 + s + 1, :]
                k_s = bk_ref[h, 1, g0 + s:g0 + s + 1, :]
                w = qg * jnp.exp2(jnp.minimum(bg - b_s, 0.0)) * k_s
                col = jnp.sum(w, axis=-1, keepdims=True)
                ag = jnp.where(place_ref[g * HG_DIAG + s] > 0.0, col, ag)
            rows.append(ag)
        accs[h] = accs[h] + jnp.concatenate(rows, axis=0)
    os = [jnp.dot(accs[h].astype(BF16), vs[h].astype(BF16), preferred_element_type=F32)
          + lax.dot_general((qs[h] * jnp.exp2(bs[h])).astype(BF16), sts[h].astype(BF16), nt,
                            preferred_element_type=F32) for h in heads]
    new = []
    for h in heads:
        bl = bs[h][c - 1:c, :]
        kd = (ks[h] * jnp.exp2(bl - bs[h])).astype(BF16)
        new.append(sts[h] * jnp.exp2(bl)
                   + jnp.dot(vs[h].T.astype(BF16), kd, preferred_element_type=F32))
    return os, new


def _hgrn_scan_kernel(q_ref, lf_ref, k_ref, v_ref, sz_ref, g_ref, tri_ref, mask_ref, place_ref,
                      y_ref, s_ref, st_ref, bk_ref):
    c = HG_CHUNK
    n_chunks = q_ref.shape[0] // c
    st_ref[...] = jnp.zeros_like(st_ref)
    cols = [slice(h * HEAD_DIM, (h + 1) * HEAD_DIM) for h in range(HG_HEADS_PER_STEP)]

    def chunk(ci, carry):
        r0 = pl.multiple_of(ci * c, c)
        rs = pl.ds(r0, c)
        os, new = _hgrn_chunks([q_ref[rs, cs] for cs in cols], [lf_ref[rs, cs] for cs in cols],
                               [k_ref[rs, cs] for cs in cols], [v_ref[rs, cs] for cs in cols],
                               [st_ref[h] for h in range(len(cols))], tri_ref[...],
                               mask_ref, place_ref, bk_ref)
        for h, cs in enumerate(cols):
            st_ref[h] = new[h]
            o = os[h]
            ms = jnp.mean(o * o, axis=-1, keepdims=True)
            y = o * lax.rsqrt(ms + RMS_EPS) * g_ref[...] * sz_ref[rs, cs]
            y_ref[rs, cs] = y.astype(y_ref.dtype)
        return carry

    lax.fori_loop(0, n_chunks, chunk, 0)
    for h in range(HG_HEADS_PER_STEP):
        s_ref[h] = st_ref[h].T


def _hgrn_place_masks():
    c = HG_CHUNK
    t = np.arange(HG_DIAG)[:, None]
    lane = np.arange(c)[None, :]
    out = [((lane == g * HG_DIAG + s) & (t >= s)) for g in range(c // HG_DIAG) for s in range(HG_DIAG)]
    return np.stack(out).astype(np.float32)


def _hgrn_scan(q, lf, k, v, sz, gnorm, batch, seq):
    m, d = q.shape
    hp = HG_HEADS_PER_STEP
    tri = jnp.asarray(np.tile(np.tril(np.ones((HG_CHUNK, HG_CHUNK), np.float32)), (1, 3)), BF16)
    masks = jnp.asarray(_hgrn_level_masks())
    place = jnp.asarray(_hgrn_place_masks())
    col = pl.BlockSpec((seq, hp * HEAD_DIM), lambda b, h: (b, h))
    const = lambda a: pl.BlockSpec(a.shape, lambda b, h: (0,) * a.ndim)
    y, s = pl.pallas_call(
        _hgrn_scan_kernel,
        out_shape=[jax.ShapeDtypeStruct((m, d), BF16),
                   jax.ShapeDtypeStruct((batch, N_HEADS, HEAD_DIM, HEAD_DIM), F32)],
        grid=(batch, N_HEADS // hp),
        in_specs=[col, col, col, col, col,
                  pl.BlockSpec((1, HEAD_DIM), lambda b, h: (0, 0)),
                  const(tri), const(masks), const(place)],
        out_specs=[col, pl.BlockSpec((None, hp, HEAD_DIM, HEAD_DIM), lambda b, h: (b, h, 0, 0))],
        scratch_shapes=[pltpu.VMEM((hp, HEAD_DIM, HEAD_DIM), F32),
                        pltpu.VMEM((hp, 2, HG_CHUNK, HEAD_DIM), F32)],
        compiler_params=_cparams(("parallel", "parallel")),
        name="hgrn_scan",
    )(q, lf, k, v, sz, gnorm.reshape(1, HEAD_DIM), tri, masks, place)
    return y, s


def _lane_bcast_col(rowvec):
    return jnp.broadcast_to(rowvec, (HEAD_DIM, HEAD_DIM)).T


def _hgrn_step_kernel(q_ref, lf_ref, k_ref, v_ref, sz_ref, g_ref, s0_ref, y_ref, s_ref):
    b = pl.program_id(0)
    lf, k, q, v, sz = lf_ref[b], k_ref[b], q_ref[b], v_ref[b], sz_ref[b]
    for h in range(N_HEADS):
        hs = slice(h, h + 1)
        f_col = _lane_bcast_col(jnp.exp2(lf[hs]))
        k_col = _lane_bcast_col(k[hs])
        q_col = _lane_bcast_col(q[hs])
        s_new = f_col * s0_ref[h] + k_col * v[hs]
        s_ref[h] = s_new
        o = jnp.sum(q_col * s_new, axis=0, keepdims=True)
        ms = jnp.mean(o * o, axis=-1, keepdims=True)
        y_ref[b, hs, :] = o * lax.rsqrt(ms + RMS_EPS) * g_ref[...] * sz[hs]


def _hgrn_step(q, lf, k, v, sz, gnorm, state_all, layer):
    db, d = q.shape
    whole = pl.BlockSpec((db, N_HEADS, HEAD_DIM), lambda b: (0, 0, 0))
    q, lf, k, v, sz = (a.reshape(db, N_HEADS, HEAD_DIM) for a in (q, lf, k, v, sz))
    y, s = pl.pallas_call(
        _hgrn_step_kernel,
        out_shape=[jax.ShapeDtypeStruct((db, N_HEADS, HEAD_DIM), F32),
                   jax.ShapeDtypeStruct((db, N_HEADS, HEAD_DIM, HEAD_DIM), F32)],
        grid=(db,),
        in_specs=[whole, whole, whole, whole, whole,
                  pl.BlockSpec((1, HEAD_DIM), lambda b: (0, 0)),
                  pl.BlockSpec((None, None, N_HEADS, HEAD_DIM, HEAD_DIM),
                               lambda b: (layer, b, 0, 0, 0))],
        out_specs=[whole, pl.BlockSpec((None, N_HEADS, HEAD_DIM, HEAD_DIM),
                                       lambda b: (b, 0, 0, 0))],
        compiler_params=_cparams(("arbitrary",)),
        name="hgrn_step",
    )(q, lf, k, v, sz, gnorm.reshape(1, HEAD_DIM), state_all)
    return y.reshape(db, d), s


def _moba_prompt_kernel(q_ref, k_ref, v_ref, sz_ref, y_ref, means_ref, kb_ref, vt_ref, qb_ref,
                        bias_ref, m_ref, l_ref, acc_ref):
    blk = MOBA_BLOCK
    nb = k_ref.shape[0] // blk
    i = pl.program_id(2)
    scale = HEAD_DIM ** -0.5
    heads = [slice(h * HEAD_DIM, (h + 1) * HEAD_DIM) for h in range(MOBA_HEADS_PER_STEP)]
    nt = (((1,), (1,)), ((), ()))

    @pl.when(i == 0)
    def _():
        for h, cs in enumerate(heads):
            for n in range(nb):
                kb = k_ref[n * blk:(n + 1) * blk, cs]
                means_ref[h, n:n + 1, :] = jnp.sum(kb, axis=0, keepdims=True) * (1.0 / blk)
                kb_ref[h, n] = kb.astype(BF16)
                vt_ref[h, n] = v_ref[n * blk:(n + 1) * blk, cs].T.astype(BF16)

    nidx = lax.broadcasted_iota(jnp.int32, (nb, blk), 0)
    past = nidx < i
    causal = (lax.broadcasted_iota(jnp.int32, (blk, blk), 0)
              <= lax.broadcasted_iota(jnp.int32, (blk, blk), 1))
    qs = [q_ref[:, cs] for cs in heads]
    sts = [lax.dot_general(means_ref[h], qs[h], nt, precision=lax.Precision.HIGHEST,
                           preferred_element_type=F32) for h in range(len(heads))]
    for h in range(len(heads)):
        qb_ref[h] = qs[h].astype(BF16)
    ss = [lax.dot_general(kb_ref[h, i], qb_ref[h], nt, preferred_element_type=F32) * scale
          for h in range(len(heads))]
    for h in range(len(heads)):
        sm = jnp.where(past, sts[h], NEG)
        rank = jnp.zeros((nb, blk), F32)
        for m in range(nb):
            rm = sm[m:m + 1, :]
            beats = (rm > sm) | ((rm == sm) & (m < nidx))
            rank = rank + jnp.where(beats, 1.0, 0.0)
        bias_ref[h] = jnp.where(past & (rank < float(MOBA_TOPK)), 0.0, NEG)
    ps = []
    for h in range(len(heads)):
        s = jnp.where(causal, ss[h], NEG)
        m0 = jnp.max(s, axis=0, keepdims=True)
        p = jnp.exp(s - m0)
        m_ref[h] = m0
        l_ref[h] = jnp.sum(p, axis=0, keepdims=True)
        ps.append(p.astype(BF16))
    for h in range(len(heads)):
        acc_ref[h] = jnp.dot(vt_ref[h, i], ps[h], preferred_element_type=F32)

    def kv_block(n, carry):
        hs = range(MOBA_HEADS_PER_STEP)
        ss = [lax.dot_general(kb_ref[h, n], qb_ref[h], nt, preferred_element_type=F32) * scale
              + bias_ref[h, pl.ds(n, 1), :] for h in hs]
        ps, alphas = [], []
        for h in hs:
            m_old = m_ref[h]
            m_new = jnp.maximum(m_old, jnp.max(ss[h], axis=0, keepdims=True))
            alpha = jnp.exp(m_old - m_new)
            p = jnp.exp(ss[h] - m_new)
            l_ref[h] = alpha * l_ref[h] + jnp.sum(p, axis=0, keepdims=True)
            m_ref[h] = m_new
            ps.append(p.astype(BF16))
            alphas.append(alpha)
        pvs = [jnp.dot(vt_ref[h, n], ps[h], preferred_element_type=F32) for h in hs]
        for h in hs:
            acc_ref[h] = alphas[h] * acc_ref[h] + pvs[h]
        return carry

    lax.fori_loop(0, i, kv_block, 0)
    for h, cs in enumerate(heads):
        o = (acc_ref[h] / l_ref[h]).T
        y_ref[:, cs] = (o * sz_ref[:, cs]).astype(y_ref.dtype)


def _moba_prompt(q, k, v, sz, batch, seq):
    m, d = q.shape
    hp = MOBA_HEADS_PER_STEP
    nq = seq // MOBA_BLOCK
    nb = seq // MOBA_BLOCK
    qspec = pl.BlockSpec((MOBA_BLOCK, hp * HEAD_DIM), lambda b, h, i: (b * nq + i, h))
    kvspec = pl.BlockSpec((seq, hp * HEAD_DIM), lambda b, h, i: (b, h))
    return pl.pallas_call(
        _moba_prompt_kernel,
        out_shape=jax.ShapeDtypeStruct((m, d), BF16),
        grid=(batch, N_HEADS // hp, nq),
        in_specs=[qspec, kvspec, kvspec, qspec],
        out_specs=qspec,
        scratch_shapes=[pltpu.VMEM((hp, nb, HEAD_DIM), F32),
                        pltpu.VMEM((hp, nb, MOBA_BLOCK, HEAD_DIM), BF16),
                        pltpu.VMEM((hp, nb, HEAD_DIM, MOBA_BLOCK), BF16),
                        pltpu.VMEM((hp, MOBA_BLOCK, HEAD_DIM), BF16),
                        pltpu.VMEM((hp, nb, MOBA_BLOCK), F32),
                        pltpu.VMEM((hp, 1, MOBA_BLOCK), F32),
                        pltpu.VMEM((hp, 1, MOBA_BLOCK), F32),
                        pltpu.VMEM((hp, HEAD_DIM, MOBA_BLOCK), F32)],
        compiler_params=_cparams(("parallel", "parallel", "arbitrary")),
        name="moba_prompt",
    )(q, k, v, sz)


def _page_sum_kernel(pt_ref, *refs):
    page_refs, o_ref = refs[:-1], refs[-1]
    ppb = MOBA_BLOCK // PAGE_SIZE
    for u in range(len(page_refs) // ppb):
        s = jnp.sum(page_refs[u * ppb][...], axis=0)
        for r in range(1, ppb):
            s = s + jnp.sum(page_refs[u * ppb + r][...], axis=0)
        o_ref[u] = s


def _block_sums(cache, page_table, layer):
    db, n_pages = page_table.shape
    ppb = MOBA_BLOCK // PAGE_SIZE
    pps = PAGES_PER_STEP
    assert n_pages % pps == 0 and pps % ppb == 0

    def page_spec(u):
        return pl.BlockSpec((None, None, PAGE_SIZE, N_HEADS, HEAD_DIM),
                            lambda b, p, pt: (layer, pt[b * n_pages + p * pps + u], 0, 0, 0))

    grid_spec = pltpu.PrefetchScalarGridSpec(
        num_scalar_prefetch=1,
        grid=(db, n_pages // pps),
        in_specs=[page_spec(u) for u in range(pps)],
        out_specs=pl.BlockSpec((None, pps // ppb, N_HEADS, HEAD_DIM),
                               lambda b, p, pt: (b, p, 0, 0)),
    )
    return pl.pallas_call(
        _page_sum_kernel,
        out_shape=jax.ShapeDtypeStruct((db, n_pages // ppb, N_HEADS, HEAD_DIM), F32),
        grid_spec=grid_spec,
        compiler_params=_cparams(("parallel", "parallel")),
        name="page_sums",
    )(page_table.reshape(-1), *([cache] * pps))


def _select_kernel(q_ref, sums_ref, idx_ref):
    nblk = sums_ref.shape[0]
    lane = lax.broadcasted_iota(jnp.int32, (nblk, LANES), 1)
    sc = jnp.full((nblk, LANES), NEG, F32)
    for h in range(N_HEADS):
        prod = sums_ref[:, h, :] * (1.0 / MOBA_BLOCK) * q_ref[h:h + 1, :]
        col = jnp.sum(prod, axis=-1, keepdims=True)
        sc = jnp.where(lane == h, col, sc)
    nidx = lax.broadcasted_iota(jnp.int32, (nblk, LANES), 0).astype(F32)
    out = jnp.zeros((SUBLANES, LANES), F32)
    orow = lax.broadcasted_iota(jnp.int32, (SUBLANES, LANES), 0)
    for t in range(MOBA_TOPK):
        mx = jnp.max(sc, axis=0, keepdims=True)
        pick = jnp.min(jnp.where(sc == mx, nidx, float(nblk)), axis=0, keepdims=True)
        out = jnp.where(orow == t, pick, out)
        sc = jnp.where(nidx == pick, -jnp.inf, sc)
    idx_ref[...] = out.astype(jnp.int32)


def _select_blocks(q, sums):
    db, d = q.shape
    nblk = sums.shape[1]
    idx = pl.pallas_call(
        _select_kernel,
        out_shape=jax.ShapeDtypeStruct((db, SUBLANES, LANES), jnp.int32),
        grid=(db,),
        in_specs=[pl.BlockSpec((None, N_HEADS, HEAD_DIM), lambda b: (b, 0, 0)),
                  pl.BlockSpec((None, nblk, N_HEADS, HEAD_DIM), lambda b: (b, 0, 0, 0))],
        out_specs=pl.BlockSpec((None, SUBLANES, LANES), lambda b: (b, 0, 0)),
        compiler_params=_cparams(("parallel",)),
        name="moba_select",
    )(q.reshape(db, N_HEADS, HEAD_DIM), sums)
    return jnp.transpose(idx[:, :MOBA_TOPK, :N_HEADS], (0, 2, 1))


def _moba_decode_kernel(idx_ref, pt_ref, q_ref, kn_ref, vn_ref, sz_ref, *refs):
    n_pg = (len(refs) - 1) // 2
    k_refs, v_refs, y_ref = refs[:n_pg], refs[n_pg:2 * n_pg], refs[-1]
    h = pl.program_id(0)
    b = pl.program_id(1)
    row = pl.ds(b * N_HEADS + h, 1)
    scale = HEAD_DIM ** -0.5
    q = q_ref[row, :]
    mine = lax.broadcasted_iota(jnp.int32, (SUBLANES, HEAD_DIM), 0) == h % SUBLANES
    qm = jnp.where(mine, q, 0.0)[None]
    s_self = (jnp.sum(q * kn_ref[row, :], axis=-1, keepdims=True) * scale)[None]
    scores = []
    mx = s_self
    for kp in k_refs:
        s = jnp.sum(jnp.sum(kp[...] * qm, axis=2, keepdims=True), axis=1, keepdims=True) * scale
        scores.append(s)
        mx = jnp.maximum(mx, jnp.max(s, axis=0, keepdims=True))
    p_self = jnp.exp(s_self - mx)
    den = p_self
    acc = jnp.where(mine, vn_ref[row, :], 0.0) * p_self[0]
    for s, vp in zip(scores, v_refs):
        p = jnp.exp(s - mx)
        den = den + jnp.sum(p, axis=0, keepdims=True)
        acc = acc + jnp.sum(p * vp[...], axis=0)
    o = jnp.sum(jnp.where(mine, acc, 0.0), axis=0, keepdims=True)
    y_ref[row, :] = o / den[0] * sz_ref[row, :]


def _moba_decode(q, k_new, v_new, sz, cache_k, cache_v, idx, page_table, layer):
    db, d = q.shape
    n_pages = page_table.shape[1]
    ppb = MOBA_BLOCK // PAGE_SIZE
    n_pg = MOBA_TOPK * ppb

    def page_spec(t):
        def page_map(h, b, idx_s, pt_s):
            blk = idx_s[(b * N_HEADS + h) * MOBA_TOPK + t // ppb]
            return (layer, pt_s[b * n_pages + blk * ppb + t % ppb], 0, h // SUBLANES, 0)
        return pl.BlockSpec((None, None, PAGE_SIZE, SUBLANES, HEAD_DIM), page_map)

    rows = db * N_HEADS
    whole = pl.BlockSpec((rows, HEAD_DIM), lambda h, b, idx_s, pt_s: (0, 0))
    pages = [page_spec(t) for t in range(n_pg)]
    grid_spec = pltpu.PrefetchScalarGridSpec(
        num_scalar_prefetch=2,
        grid=(N_HEADS, db),
        in_specs=[whole, whole, whole, whole] + pages + pages,
        out_specs=whole,
    )
    as_rows = lambda a: a.reshape(rows, HEAD_DIM)
    y = pl.pallas_call(
        _moba_decode_kernel,
        out_shape=jax.ShapeDtypeStruct((rows, HEAD_DIM), F32),
        grid_spec=grid_spec,
        compiler_params=_cparams(("arbitrary", "arbitrary")),
        name="moba_decode",
    )(idx.reshape(-1), page_table.reshape(-1), as_rows(q), as_rows(k_new), as_rows(v_new),
      as_rows(sz), *([cache_k] * n_pg), *([cache_v] * n_pg))
    return y.reshape(db, d)


def _row_tile(m):
    return 256 if m % 256 == 0 else m


def kernel(x_prompt, x_sample, cache_k, cache_v, state_hgrn, page_table, norm_w, w_in, w_out,
           hgrn_lower_bounds, hgrn_gnorm, final_norm_w):
    batch, seq, d = x_prompt.shape
    db, dq, _ = x_sample.shape
    depth = norm_w.shape[0]
    past_len = page_table.shape[1] * PAGE_SIZE
    assert dq == 1 and d == N_HEADS * HEAD_DIM and seq % MOBA_BLOCK == 0 and seq % HG_CHUNK == 0
    assert past_len % MOBA_BLOCK == 0 and past_len // MOBA_BLOCK >= MOBA_TOPK

    w_in_b = w_in.astype(BF16)
    w_out_b = w_out.astype(BF16)
    cos_p, sin_p = _rope_tables(jnp.arange(seq, dtype=F32))
    cos_s, sin_s = _rope_tables(jnp.arange(dq, dtype=F32) + past_len)
    cos_s = jnp.broadcast_to(cos_s, (db * dq, HEAD_DIM))
    sin_s = jnp.broadcast_to(sin_s, (db * dq, HEAD_DIM))

    xp = x_prompt.reshape(batch * seq, d)
    xs = x_sample.reshape(db * dq, d)
    tp = _row_tile(xp.shape[0])
    ts = _row_tile(xs.shape[0])
    xnp = _rmsnorm(xp, norm_w[0], tp)
    xns = _rmsnorm(xs, norm_w[0], ts)

    hg_p, hg_s, kp_l, vp_l, ks_l, vs_l = [], [], [], [], [], []
    for i in range(depth):
        j = i // 2
        if i % 2 == 0:
            q, lf, k, v, sz = _proj_hgrn(xnp, w_in_b, i, hgrn_lower_bounds, j, tp)
            yp, sp = _hgrn_scan(q, lf, k, v, sz, hgrn_gnorm[j], batch, seq)
            q, lf, k, v, sz = _proj_hgrn(xns, w_in_b, i, hgrn_lower_bounds, j, ts)
            ys, ss = _hgrn_step(q, lf, k, v, sz, hgrn_gnorm[j], state_hgrn, j)
            hg_p.append(sp)
            hg_s.append(ss)
        else:
            q, k, v, sz = _proj_moba(xnp, w_in_b, i, cos_p, sin_p, tp)
            yp = _moba_prompt(q, k, v, sz, batch, seq)
            kp_l.append(k)
            vp_l.append(v)
            q, k, v, sz = _proj_moba(xns, w_in_b, i, cos_s, sin_s, ts)
            sums = _block_sums(cache_k, page_table, j)
            idx = _select_blocks(q, sums)
            ys = _moba_decode(q, k, v, sz, cache_k, cache_v, idx, page_table, j)
            ks_l.append(k)
            vs_l.append(v)
        last = i == depth - 1
        nw = final_norm_w if last else norm_w[i + 1]
        outp = _oproj(yp, w_out_b, i, xp, nw, tp, last)
        outs = _oproj(ys, w_out_b, i, xs, nw, ts, last)
        if last:
            y_prompt, y_sample = outp[0], outs[0]
        else:
            xp, xnp = outp
            xs, xns = outs

    shp = (batch, seq, N_HEADS, HEAD_DIM)
    shs = (db, dq, N_HEADS, HEAD_DIM)
    return (y_prompt.reshape(batch, seq, d),
            y_sample.reshape(db, dq, d),
            jnp.stack(hg_p),
            jnp.stack([t.reshape(shp) for t in kp_l]),
            jnp.stack([t.reshape(shp) for t in vp_l]),
            jnp.stack(hg_s),
            jnp.stack([t.reshape(shs) for t in ks_l]),
            jnp.stack([t.reshape(shs) for t in vs_l]))
```

```python
import functools
import math

import numpy as np
import jax
import jax.numpy as jnp
from jax import lax
from jax.experimental import pallas as pl
from jax.experimental.pallas import tpu as pltpu

F32 = jnp.float32
BF16 = jnp.bfloat16

LANES = 128
SUBLANES = 8
VMEM_LIMIT_BYTES = 56 * 1024 * 1024

N_HEADS = 16
HEAD_DIM = 128
PAGE_SIZE = 128
MOBA_BLOCK = 256
MOBA_TOPK = 3
ROPE_THETA = 10000.0
RMS_EPS = 1e-6
NEG = -1e30
LB_FLOOR = 1e-30

HG_CHUNK = 128
HG_DIAG = 8
HG_LEVELS = (8, 16, 32, 64)
HG_HEADS_PER_STEP = 4
HG_SEQ_PARTS = 2
MOBA_HEADS_PER_STEP = 4


def _cparams(sem):
    return pltpu.CompilerParams(dimension_semantics=sem, vmem_limit_bytes=VMEM_LIMIT_BYTES)


def _silu(x):
    return x / (1.0 + jnp.exp(-x))


def _rms_kernel(x_ref, w_ref, o_ref):
    x = x_ref[...]
    ms = jnp.mean(x * x, axis=-1, keepdims=True)
    o_ref[...] = (x * lax.rsqrt(ms + RMS_EPS) * w_ref[...]).astype(o_ref.dtype)


def _rmsnorm(x, w, tm):
    m, d = x.shape
    return pl.pallas_call(
        _rms_kernel,
        out_shape=jax.ShapeDtypeStruct((m, d), BF16),
        grid=(m // tm,),
        in_specs=[pl.BlockSpec((tm, d), lambda i: (i, 0)),
                  pl.BlockSpec((1, d), lambda i: (0, 0))],
        out_specs=pl.BlockSpec((tm, d), lambda i: (i, 0)),
        compiler_params=_cparams(("parallel",)),
        name="rmsnorm",
    )(x, w.reshape(1, d))


def _epi_plain(acc, extra, outs):
    outs[0][...] = acc


def _epi_silu(acc, extra, outs):
    outs[0][...] = _silu(acc)


def _epi_rope(acc, extra, outs):
    cos_ref, sin_ref = extra
    cos = cos_ref[...]
    sin = sin_ref[...]
    for h in range(N_HEADS):
        xh = acc[:, h * HEAD_DIM:(h + 1) * HEAD_DIM]
        outs[0][:, h * HEAD_DIM:(h + 1) * HEAD_DIM] = (
            xh * cos + pltpu.roll(xh, HEAD_DIM // 2, 1) * sin)


def _hgrn_lower_bound(lbw, layer):
    n = lbw.shape[0]
    rows = [lbw[r:r + 1, :] for r in range(n)]
    mx = rows[0]
    for r in rows[1:]:
        mx = jnp.maximum(mx, r)
    es = [jnp.exp(r - mx) for r in rows]
    tot = es[0]
    for e in es[1:]:
        tot = tot + e
    sm = [e / tot for e in es]
    cs = sm[0]
    for r in range(1, layer + 1):
        cs = cs + sm[r]
    return cs - sm[0]


def _epi_gates(acc, extra, outs, *, layer):
    lb = _hgrn_lower_bound(extra[0][...], layer)
    lb_floor = jnp.maximum(lb, LB_FLOOR)
    one_m = 1.0 - lb
    z = acc
    t = jnp.exp(-jnp.abs(z))
    r = 1.0 / (1.0 + t)
    pos = z >= 0.0
    sig = jnp.where(pos, r, t * r)
    nsig = jnp.where(pos, t * r, r)
    outs[0][...] = jnp.log2(lb_floor + one_m * sig)
    outs[1][...] = one_m * nsig


def _proj_kernel(*refs, epilogue, n_extra, n_side, has_prev):
    if n_side:
        refs = refs[1:]
    x_ref, w_ref = refs[0], refs[1]
    extra = refs[2:2 + n_extra]
    pos = 2 + n_extra
    pages = refs[pos:pos + n_side]
    pos += n_side + (1 if has_prev else 0)
    outs = refs[pos:]
    acc = jnp.dot(x_ref[...], w_ref[...], preferred_element_type=F32)
    if n_side:
        outs, sums_ref = outs[:-1], outs[-1]
        ppb = MOBA_BLOCK // PAGE_SIZE
        for u in range(n_side // ppb):
            s = jnp.sum(pages[u * ppb][...], axis=0)
            for r in range(1, ppb):
                s = s + jnp.sum(pages[u * ppb + r][...], axis=0)
            sums_ref[u] = s
    epilogue(acc, extra, outs)


def _proj(xn, w, layer, group, epilogue, extra, extra_specs, n_out, tm, name, side=None, stack=None):
    m, d = xn.shape
    steps = m // tm
    operands = [xn, w, *extra]
    in_specs = [pl.BlockSpec((tm, d), lambda i, *_: (i, 0)),
                pl.BlockSpec((None, d, d), lambda i, *_: (layer, 0, group))] + list(extra_specs)
    out_shape = [jax.ShapeDtypeStruct((m, d), F32)] * n_out
    out_specs = [pl.BlockSpec((tm, d), lambda i, *_: (i, 0))] * n_out
    prefetch, n_side, aliases = [], 0, {}
    if side is not None:
        cache, page_table, moba_layer, seq = side
        n_pages = page_table.shape[1]
        ppb = MOBA_BLOCK // PAGE_SIZE
        n_side = n_pages // steps
        assert n_side * steps == n_pages and n_side % ppb == 0
        prefetch = [page_table.reshape(-1)]

        def page_spec(u):
            return pl.BlockSpec(
                (None, None, PAGE_SIZE, N_HEADS, HEAD_DIM),
                lambda i, pt: (moba_layer, pt[seq * n_pages + i * n_side + u], 0, 0, 0))

        operands += [cache] * n_side
        in_specs += [page_spec(u) for u in range(n_side)]
        out_shape = out_shape + [jax.ShapeDtypeStruct((n_pages // ppb, N_HEADS, HEAD_DIM), F32)]
        out_specs = out_specs + [pl.BlockSpec((n_side // ppb, N_HEADS, HEAD_DIM),
                                              lambda i, *_: (i, 0, 0))]
    if stack is not None:
        slab, n_slabs, prev = stack
        out_shape = [jax.ShapeDtypeStruct((n_slabs, m, d), F32)] + out_shape[1:]
        out_specs = [pl.BlockSpec((None, tm, d), lambda i, *_: (slab, i, 0))] + out_specs[1:]
        if prev is not None:
            aliases = {len(prefetch) + len(operands): 0}
            operands.append(prev)
            in_specs.append(pl.BlockSpec(memory_space=pl.ANY))
    kern = functools.partial(_proj_kernel, epilogue=epilogue, n_extra=len(extra), n_side=n_side,
                             has_prev=stack is not None and stack[2] is not None)
    return pl.pallas_call(
        kern,
        out_shape=out_shape,
        grid_spec=pltpu.PrefetchScalarGridSpec(
            num_scalar_prefetch=len(prefetch), grid=(steps,), in_specs=in_specs,
            out_specs=out_specs),
        input_output_aliases=aliases,
        compiler_params=_cparams(("parallel",)),
        name=name,
    )(*prefetch, *operands)


def _rope_tables(pos):
    half = HEAD_DIM // 2
    inv = jnp.power(ROPE_THETA, -jnp.arange(half, dtype=F32) / half)
    ang = pos[:, None] * inv[None, :]
    cos = jnp.cos(ang)
    sin = jnp.sin(ang)
    return jnp.concatenate([cos, cos], axis=-1), jnp.concatenate([-sin, sin], axis=-1)


def _proj_hgrn(xn, w, layer, lbw, hg_layer, tm, sides=(None,) * 4):
    q, *s0 = _proj(xn, w, layer, 0, _epi_silu, (), (), 1, tm, "proj_hgrn_q", side=sides[0])
    lf, k, *s1 = _proj(xn, w, layer, 1, functools.partial(_epi_gates, layer=hg_layer), (lbw,),
                       (pl.BlockSpec(lbw.shape, lambda i, *_: (0, 0)),), 2, tm, "proj_hgrn_f",
                       side=sides[1])
    v, *s2 = _proj(xn, w, layer, 2, _epi_plain, (), (), 1, tm, "proj_hgrn_i", side=sides[2])
    sz, *s3 = _proj(xn, w, layer, 3, _epi_silu, (), (), 1, tm, "proj_z", side=sides[3])
    return (q, lf, k, v, sz), s0 + s1 + s2 + s3


def _proj_moba(xn, w, layer, cos, sin, tm, sides=(None,) * 4, stacks=(None, None)):
    nblk = cos.shape[0] // tm
    tspec = pl.BlockSpec((tm, HEAD_DIM), lambda i, *_: (i % nblk, 0))
    q, *s0 = _proj(xn, w, layer, 0, _epi_rope, (cos, sin), (tspec, tspec), 1, tm, "proj_moba_q",
                   side=sides[0])
    k, *s1 = _proj(xn, w, layer, 1, _epi_rope, (cos, sin), (tspec, tspec), 1, tm, "proj_moba_k",
                   side=sides[1], stack=stacks[0])
    v, *s2 = _proj(xn, w, layer, 2, _epi_plain, (), (), 1, tm, "proj_moba_v", side=sides[2],
                   stack=stacks[1])
    sz, *s3 = _proj(xn, w, layer, 3, _epi_silu, (), (), 1, tm, "proj_z", side=sides[3])
    return (q, k, v, sz), s0 + s1 + s2 + s3


def _oproj_kernel(y_ref, w_ref, x_ref, nw_ref, *outs, emit_x):
    xo = x_ref[...] + jnp.dot(y_ref[...].astype(BF16), w_ref[...], preferred_element_type=F32)
    ms = jnp.mean(xo * xo, axis=-1, keepdims=True)
    xn = xo * lax.rsqrt(ms + RMS_EPS) * nw_ref[...]
    if emit_x:
        outs[0][...] = xo
        outs[1][...] = xn.astype(outs[1].dtype)
    else:
        outs[0][...] = xn.astype(outs[0].dtype)


def _oproj(y, w, layer, x, nw, tm, last):
    m, d = x.shape
    row = pl.BlockSpec((tm, d), lambda i: (i, 0))
    if last:
        out_shape = [jax.ShapeDtypeStruct((m, d), F32)]
    else:
        out_shape = [jax.ShapeDtypeStruct((m, d), F32), jax.ShapeDtypeStruct((m, d), BF16)]
    return pl.pallas_call(
        functools.partial(_oproj_kernel, emit_x=not last),
        out_shape=out_shape,
        grid=(m // tm,),
        in_specs=[row, pl.BlockSpec((None, d, d), lambda i: (layer, 0, 0)), row,
                  pl.BlockSpec((1, d), lambda i: (0, 0))],
        out_specs=[row] * len(out_shape),
        compiler_params=_cparams(("parallel",)),
        name="oproj",
    )(y, w, x, nw.reshape(1, d))


def _hgrn_level_masks():
    c = HG_CHUNK
    t = np.arange(c)[:, None]
    s = np.arange(c)[None, :]
    masks = []
    for m in HG_LEVELS:
        masks.append(((t // (2 * m) == s // (2 * m)) & (t % (2 * m) >= m) & (s % (2 * m) < m)))
    return np.stack(masks).astype(np.float32)


def _split3(x):
    hi = x.astype(BF16)
    r1 = x - hi.astype(F32)
    mid = r1.astype(BF16)
    lo = (r1 - mid.astype(F32)).astype(BF16)
    return hi, mid, lo


def _hgrn_chunks(qs, lfs, ks, vs, sts, tri3, mask_ref, place_ref, bk_ref):
    c = HG_CHUNK
    heads = range(len(qs))
    nt = (((1,), (1,)), ((), ()))
    bs = [jnp.dot(tri3, jnp.concatenate(_split3(lfs[h]), axis=0), preferred_element_type=F32)
          for h in heads]
    accs = [jnp.zeros((c, c), F32) for _ in heads]
    for li, m in enumerate(HG_LEVELS):
        ams = []
        for h in heads:
            b = bs[h]
            pieces = []
            for blk in range(c // (2 * m)):
                bnd = blk * 2 * m + m - 1
                pieces.append(jnp.broadcast_to(b[bnd:bnd + 1, :], (2 * m, HEAD_DIM)))
            ref_b = pieces[0] if len(pieces) == 1 else jnp.concatenate(pieces, axis=0)
            d = b - ref_b
            e = jnp.exp2(jnp.minimum(d, -d))
            ams.append(lax.dot_general((qs[h] * e).astype(BF16), (ks[h] * e).astype(BF16), nt,
                                       preferred_element_type=F32))
        for h in heads:
            accs[h] = accs[h] + ams[h] * mask_ref[li]
    for h in heads:
        b = bs[h]
        bk_ref[h, 0] = b
        bk_ref[h, 1] = ks[h]
        rows = []
        for g in range(c // HG_DIAG):
            g0 = g * HG_DIAG
            qg = qs[h][g0:g0 + HG_DIAG, :]
            bg = b[g0:g0 + HG_DIAG, :]
            ag = jnp.zeros((HG_DIAG, c), F32)
            for s in range(HG_DIAG):
                b_s = bk_ref[h, 0, g0 + s:g0 + s + 1, :]
                k_s = bk_ref[h, 1, g0 + s:g0 + s + 1, :]
                w = qg * jnp.exp2(jnp.minimum(bg - b_s, 0.0)) * k_s
                col = jnp.sum(w, axis=-1, keepdims=True)
                ag = jnp.where(place_ref[g * HG_DIAG + s] > 0.0, col, ag)
            rows.append(ag)
        accs[h] = accs[h] + jnp.concatenate(rows, axis=0)
    os = [jnp.dot(accs[h].astype(BF16), vs[h].astype(BF16), preferred_element_type=F32)
          + lax.dot_general((qs[h] * jnp.exp2(bs[h])).astype(BF16), sts[h].astype(BF16), nt,
                            preferred_element_type=F32) for h in heads]
    new = []
    for h in heads:
        bl = bs[h][c - 1:c, :]
        kd = (ks[h] * jnp.exp2(bl - bs[h])).astype(BF16)
        new.append(sts[h] * jnp.exp2(bl)
                   + jnp.dot(vs[h].T.astype(BF16), kd, preferred_element_type=F32))
    return os, new


def _hgrn_scan_kernel(q_ref, lf_ref, k_ref, v_ref, sz_ref, g_ref, tri_ref, mask_ref, place_ref,
                      y_ref, s_ref, st_ref, bk_ref):
    c = HG_CHUNK
    n_chunks = q_ref.shape[0] // c
    part = pl.program_id(2)

    @pl.when(part == 0)
    def _():
        st_ref[...] = jnp.zeros_like(st_ref)

    cols = [slice(h * HEAD_DIM, (h + 1) * HEAD_DIM) for h in range(HG_HEADS_PER_STEP)]

    def chunk(ci, carry):
        r0 = pl.multiple_of(ci * c, c)
        rs = pl.ds(r0, c)
        os, new = _hgrn_chunks([q_ref[rs, cs] for cs in cols], [lf_ref[rs, cs] for cs in cols],
                               [k_ref[rs, cs] for cs in cols], [v_ref[rs, cs] for cs in cols],
                               [st_ref[h] for h in range(len(cols))], tri_ref[...],
                               mask_ref, place_ref, bk_ref)
        for h, cs in enumerate(cols):
            st_ref[h] = new[h]
            o = os[h]
            ms = jnp.mean(o * o, axis=-1, keepdims=True)
            y = o * lax.rsqrt(ms + RMS_EPS) * g_ref[...] * sz_ref[rs, cs]
            y_ref[rs, cs] = y.astype(y_ref.dtype)
        return carry

    lax.fori_loop(0, n_chunks, chunk, 0)

    @pl.when(part == pl.num_programs(2) - 1)
    def _():
        for h in range(HG_HEADS_PER_STEP):
            s_ref[h] = st_ref[h].T


def _hgrn_place_masks():
    c = HG_CHUNK
    t = np.arange(HG_DIAG)[:, None]
    lane = np.arange(c)[None, :]
    out = [((lane == g * HG_DIAG + s) & (t >= s)) for g in range(c // HG_DIAG) for s in range(HG_DIAG)]
    return np.stack(out).astype(np.float32)


def _hgrn_scan(q, lf, k, v, sz, gnorm, batch, seq):
    m, d = q.shape
    hp = HG_HEADS_PER_STEP
    tri = jnp.asarray(np.tile(np.tril(np.ones((HG_CHUNK, HG_CHUNK), np.float32)), (1, 3)), BF16)
    masks = jnp.asarray(_hgrn_level_masks())
    place = jnp.asarray(_hgrn_place_masks())
    parts = HG_SEQ_PARTS
    assert seq % (parts * HG_CHUNK) == 0
    col = pl.BlockSpec((seq // parts, hp * HEAD_DIM), lambda b, h, p: (b * parts + p, h))
    const = lambda a: pl.BlockSpec(a.shape, lambda b, h, p: (0,) * a.ndim)
    y, s = pl.pallas_call(
        _hgrn_scan_kernel,
        out_shape=[jax.ShapeDtypeStruct((m, d), BF16),
                   jax.ShapeDtypeStruct((batch, N_HEADS, HEAD_DIM, HEAD_DIM), F32)],
        grid=(batch, N_HEADS // hp, parts),
        in_specs=[col, col, col, col, col,
                  pl.BlockSpec((1, HEAD_DIM), lambda b, h, p: (0, 0)),
                  const(tri), const(masks), const(place)],
        out_specs=[col, pl.BlockSpec((None, hp, HEAD_DIM, HEAD_DIM),
                                     lambda b, h, p: (b, h, 0, 0))],
        scratch_shapes=[pltpu.VMEM((hp, HEAD_DIM, HEAD_DIM), F32),
                        pltpu.VMEM((hp, 2, HG_CHUNK, HEAD_DIM), F32)],
        compiler_params=_cparams(("parallel", "parallel", "arbitrary")),
        name="hgrn_scan",
    )(q, lf, k, v, sz, gnorm.reshape(1, HEAD_DIM), tri, masks, place)
    return y, s


def _lane_bcast_col(rowvec):
    return jnp.broadcast_to(rowvec, (HEAD_DIM, HEAD_DIM)).T


def _hgrn_step_kernel(q_ref, lf_ref, k_ref, v_ref, sz_ref, g_ref, s0_ref, y_ref, s_ref):
    b = pl.program_id(0)
    lf, k, q, v, sz = lf_ref[b], k_ref[b], q_ref[b], v_ref[b], sz_ref[b]
    for h in range(N_HEADS):
        hs = slice(h, h + 1)
        f_col = _lane_bcast_col(jnp.exp2(lf[hs]))
        k_col = _lane_bcast_col(k[hs])
        q_col = _lane_bcast_col(q[hs])
        s_new = f_col * s0_ref[h] + k_col * v[hs]
        s_ref[h] = s_new
        o = jnp.sum(q_col * s_new, axis=0, keepdims=True)
        ms = jnp.mean(o * o, axis=-1, keepdims=True)
        y_ref[b, hs, :] = o * lax.rsqrt(ms + RMS_EPS) * g_ref[...] * sz[hs]


def _hgrn_step(q, lf, k, v, sz, gnorm, state_all, layer):
    db, d = q.shape
    whole = pl.BlockSpec((db, N_HEADS, HEAD_DIM), lambda b: (0, 0, 0))
    q, lf, k, v, sz = (a.reshape(db, N_HEADS, HEAD_DIM) for a in (q, lf, k, v, sz))
    y, s = pl.pallas_call(
        _hgrn_step_kernel,
        out_shape=[jax.ShapeDtypeStruct((db, N_HEADS, HEAD_DIM), F32),
                   jax.ShapeDtypeStruct((db, N_HEADS, HEAD_DIM, HEAD_DIM), F32)],
        grid=(db,),
        in_specs=[whole, whole, whole, whole, whole,
                  pl.BlockSpec((1, HEAD_DIM), lambda b: (0, 0)),
                  pl.BlockSpec((None, None, N_HEADS, HEAD_DIM, HEAD_DIM),
                               lambda b: (layer, b, 0, 0, 0))],
        out_specs=[whole, pl.BlockSpec((None, N_HEADS, HEAD_DIM, HEAD_DIM),
                                       lambda b: (b, 0, 0, 0))],
        compiler_params=_cparams(("arbitrary",)),
        name="hgrn_step",
    )(q, lf, k, v, sz, gnorm.reshape(1, HEAD_DIM), state_all)
    return y.reshape(db, d), s


def _moba_prompt_kernel(q_ref, k_ref, v_ref, sz_ref, y_ref, means_ref, kb_ref, vt_ref, qb_ref,
                        bias_ref, m_ref, l_ref, acc_ref):
    blk = MOBA_BLOCK
    nb = k_ref.shape[0] // blk
    i = pl.program_id(2)
    scale = HEAD_DIM ** -0.5
    heads = [slice(h * HEAD_DIM, (h + 1) * HEAD_DIM) for h in range(MOBA_HEADS_PER_STEP)]
    nt = (((1,), (1,)), ((), ()))

    @pl.when(i == 0)
    def _():
        for h, cs in enumerate(heads):
            for n in range(nb):
                kb = k_ref[n * blk:(n + 1) * blk, cs]
                means_ref[h, n:n + 1, :] = jnp.sum(kb, axis=0, keepdims=True) * (1.0 / blk)
                kb_ref[h, n] = kb.astype(BF16)
                vt_ref[h, n] = v_ref[n * blk:(n + 1) * blk, cs].T.astype(BF16)

    nidx = lax.broadcasted_iota(jnp.int32, (nb, blk), 0)
    past = nidx < i
    causal = (lax.broadcasted_iota(jnp.int32, (blk, blk), 0)
              <= lax.broadcasted_iota(jnp.int32, (blk, blk), 1))
    qs = [q_ref[:, cs] for cs in heads]
    sts = [lax.dot_general(means_ref[h], qs[h], nt, precision=lax.Precision.HIGHEST,
                           preferred_element_type=F32) for h in range(len(heads))]
    for h in range(len(heads)):
        qb_ref[h] = qs[h].astype(BF16)
    ss = [lax.dot_general(kb_ref[h, i], qb_ref[h], nt, preferred_element_type=F32) * scale
          for h in range(len(heads))]
    for h in range(len(heads)):
        sm = jnp.where(past, sts[h], NEG)
        rank = jnp.zeros((nb, blk), F32)
        for m in range(nb):
            rm = sm[m:m + 1, :]
            beats = (rm > sm) | ((rm == sm) & (m < nidx))
            rank = rank + jnp.where(beats, 1.0, 0.0)
        bias_ref[h] = jnp.where(past & (rank < float(MOBA_TOPK)), 0.0, NEG)
    ps = []
    for h in range(len(heads)):
        s = jnp.where(causal, ss[h], NEG)
        m0 = jnp.max(s, axis=0, keepdims=True)
        p = jnp.exp(s - m0)
        m_ref[h] = m0
        l_ref[h] = jnp.sum(p, axis=0, keepdims=True)
        ps.append(p.astype(BF16))
    for h in range(len(heads)):
        acc_ref[h] = jnp.dot(vt_ref[h, i], ps[h], preferred_element_type=F32)

    def kv_block(n, carry):
        hs = range(MOBA_HEADS_PER_STEP)
        ss = [lax.dot_general(kb_ref[h, n], qb_ref[h], nt, preferred_element_type=F32) * scale
              + bias_ref[h, pl.ds(n, 1), :] for h in hs]
        ps, alphas = [], []
        for h in hs:
            m_old = m_ref[h]
            m_new = jnp.maximum(m_old, jnp.max(ss[h], axis=0, keepdims=True))
            alpha = jnp.exp(m_old - m_new)
            p = jnp.exp(ss[h] - m_new)
            l_ref[h] = alpha * l_ref[h] + jnp.sum(p, axis=0, keepdims=True)
            m_ref[h] = m_new
            ps.append(p.astype(BF16))
            alphas.append(alpha)
        pvs = [jnp.dot(vt_ref[h, n], ps[h], preferred_element_type=F32) for h in hs]
        for h in hs:
            acc_ref[h] = alphas[h] * acc_ref[h] + pvs[h]
        return carry

    lax.fori_loop(0, i, kv_block, 0)
    for h, cs in enumerate(heads):
        o = (acc_ref[h] / l_ref[h]).T
        y_ref[:, cs] = (o * sz_ref[:, cs]).astype(y_ref.dtype)


def _moba_prompt(q, k, v, sz, batch, seq, slab=None):
    m, d = q.shape
    hp = MOBA_HEADS_PER_STEP
    nq = seq // MOBA_BLOCK
    nb = seq // MOBA_BLOCK
    qspec = pl.BlockSpec((MOBA_BLOCK, hp * HEAD_DIM), lambda b, h, i: (b * nq + i, h))
    if slab is None:
        kvspec = pl.BlockSpec((seq, hp * HEAD_DIM), lambda b, h, i: (b, h))
    else:
        kvspec = pl.BlockSpec((None, seq, hp * HEAD_DIM), lambda b, h, i: (slab, b, h))
    return pl.pallas_call(
        _moba_prompt_kernel,
        out_shape=jax.ShapeDtypeStruct((m, d), BF16),
        grid=(batch, N_HEADS // hp, nq),
        in_specs=[qspec, kvspec, kvspec, qspec],
        out_specs=qspec,
        scratch_shapes=[pltpu.VMEM((hp, nb, HEAD_DIM), F32),
                        pltpu.VMEM((hp, nb, MOBA_BLOCK, HEAD_DIM), BF16),
                        pltpu.VMEM((hp, nb, HEAD_DIM, MOBA_BLOCK), BF16),
                        pltpu.VMEM((hp, MOBA_BLOCK, HEAD_DIM), BF16),
                        pltpu.VMEM((hp, nb, MOBA_BLOCK), F32),
                        pltpu.VMEM((hp, 1, MOBA_BLOCK), F32),
                        pltpu.VMEM((hp, 1, MOBA_BLOCK), F32),
                        pltpu.VMEM((hp, HEAD_DIM, MOBA_BLOCK), F32)],
        compiler_params=_cparams(("parallel", "parallel", "arbitrary")),
        name="moba_prompt",
    )(q, k, v, sz)


def _select_kernel(q_ref, sums_ref, idx_ref):
    nblk = sums_ref.shape[0]
    lane = lax.broadcasted_iota(jnp.int32, (nblk, LANES), 1)
    sc = jnp.full((nblk, LANES), NEG, F32)
    for h in range(N_HEADS):
        prod = sums_ref[:, h, :] * (1.0 / MOBA_BLOCK) * q_ref[h:h + 1, :]
        col = jnp.sum(prod, axis=-1, keepdims=True)
        sc = jnp.where(lane == h, col, sc)
    nidx = lax.broadcasted_iota(jnp.int32, (nblk, LANES), 0).astype(F32)
    out = jnp.zeros((SUBLANES, LANES), F32)
    orow = lax.broadcasted_iota(jnp.int32, (SUBLANES, LANES), 0)
    for t in range(MOBA_TOPK):
        mx = jnp.max(sc, axis=0, keepdims=True)
        pick = jnp.min(jnp.where(sc == mx, nidx, float(nblk)), axis=0, keepdims=True)
        out = jnp.where(orow == t, pick, out)
        sc = jnp.where(nidx == pick, -jnp.inf, sc)
    idx_ref[...] = out.astype(jnp.int32)


def _select_blocks(q, sums):
    db, d = q.shape
    nblk = sums.shape[1]
    idx = pl.pallas_call(
        _select_kernel,
        out_shape=jax.ShapeDtypeStruct((db, SUBLANES, LANES), jnp.int32),
        grid=(db,),
        in_specs=[pl.BlockSpec((None, N_HEADS, HEAD_DIM), lambda b: (b, 0, 0)),
                  pl.BlockSpec((None, nblk, N_HEADS, HEAD_DIM), lambda b: (b, 0, 0, 0))],
        out_specs=pl.BlockSpec((None, SUBLANES, LANES), lambda b: (b, 0, 0)),
        compiler_params=_cparams(("parallel",)),
        name="moba_select",
    )(q.reshape(db, N_HEADS, HEAD_DIM), sums)
    return jnp.transpose(idx[:, :MOBA_TOPK, :N_HEADS], (0, 2, 1))


def _moba_decode_kernel(idx_ref, pt_ref, q_ref, kn_ref, vn_ref, sz_ref, *refs):
    n_pg = (len(refs) - 1) // 2
    k_refs, v_refs, y_ref = refs[:n_pg], refs[n_pg:2 * n_pg], refs[-1]
    h = pl.program_id(0)
    b = pl.program_id(1)
    row = pl.ds(b * N_HEADS + h, 1)
    scale = HEAD_DIM ** -0.5
    q = q_ref[row, :]
    mine = lax.broadcasted_iota(jnp.int32, (SUBLANES, HEAD_DIM), 0) == h % SUBLANES
    mine_col = (lax.broadcasted_iota(jnp.int32, (1, SUBLANES, 1), 1) == h % SUBLANES)
    s_self = (jnp.sum(q * kn_ref[row, :], axis=-1, keepdims=True) * scale)[None]
    scores = []
    mx8 = jnp.full((1, SUBLANES, 1), -jnp.inf, F32)
    for kp in k_refs:
        s = jnp.sum(kp[...] * q[None], axis=2, keepdims=True) * scale
        s = jnp.where(mine_col, s, -jnp.inf)
        scores.append(s)
        mx8 = jnp.maximum(mx8, jnp.max(s, axis=0, keepdims=True))
    mx = jnp.maximum(s_self, jnp.max(mx8, axis=1, keepdims=True))
    p_self = jnp.exp(s_self - mx)
    den8 = jnp.zeros((1, SUBLANES, 1), F32)
    acc = jnp.where(mine, vn_ref[row, :], 0.0) * p_self[0]
    for s, vp in zip(scores, v_refs):
        p = jnp.exp(s - mx)
        den8 = den8 + jnp.sum(p, axis=0, keepdims=True)
        acc = acc + jnp.sum(p * vp[...], axis=0)
    den = p_self + jnp.sum(den8, axis=1, keepdims=True)
    o = jnp.sum(jnp.where(mine, acc, 0.0), axis=0, keepdims=True)
    y_ref[row, :] = o / den[0] * sz_ref[row, :]


def _moba_decode(q, k_new, v_new, sz, cache_k, cache_v, idx, page_table, layer):
    db, d = q.shape
    n_pages = page_table.shape[1]
    ppb = MOBA_BLOCK // PAGE_SIZE
    n_pg = MOBA_TOPK * ppb

    def page_spec(t):
        def page_map(h, b, idx_s, pt_s):
            blk = idx_s[(b * N_HEADS + h) * MOBA_TOPK + t // ppb]
            return (layer, pt_s[b * n_pages + blk * ppb + t % ppb], 0, h // SUBLANES, 0)
        return pl.BlockSpec((None, None, PAGE_SIZE, SUBLANES, HEAD_DIM), page_map)

    rows = db * N_HEADS
    whole = pl.BlockSpec((rows, HEAD_DIM), lambda h, b, idx_s, pt_s: (0, 0))
    pages = [page_spec(t) for t in range(n_pg)]
    grid_spec = pltpu.PrefetchScalarGridSpec(
        num_scalar_prefetch=2,
        grid=(N_HEADS, db),
        in_specs=[whole, whole, whole, whole] + pages + pages,
        out_specs=whole,
    )
    as_rows = lambda a: a.reshape(rows, HEAD_DIM)
    y = pl.pallas_call(
        _moba_decode_kernel,
        out_shape=jax.ShapeDtypeStruct((rows, HEAD_DIM), F32),
        grid_spec=grid_spec,
        compiler_params=_cparams(("arbitrary", "arbitrary")),
        name="moba_decode",
    )(idx.reshape(-1), page_table.reshape(-1), as_rows(q), as_rows(k_new), as_rows(v_new),
      as_rows(sz), *([cache_k] * n_pg), *([cache_v] * n_pg))
    return y.reshape(db, d)


def _row_tile(m):
    return 256 if m % 256 == 0 else m


def kernel(x_prompt, x_sample, cache_k, cache_v, state_hgrn, page_table, norm_w, w_in, w_out,
           hgrn_lower_bounds, hgrn_gnorm, final_norm_w):
    batch, seq, d = x_prompt.shape
    db, dq, _ = x_sample.shape
    depth = norm_w.shape[0]
    past_len = page_table.shape[1] * PAGE_SIZE
    assert dq == 1 and d == N_HEADS * HEAD_DIM and seq % MOBA_BLOCK == 0 and seq % HG_CHUNK == 0
    assert past_len % MOBA_BLOCK == 0 and past_len // MOBA_BLOCK >= MOBA_TOPK

    w_in_b = w_in.astype(BF16)
    w_out_b = w_out.astype(BF16)
    cos_p, sin_p = _rope_tables(jnp.arange(seq, dtype=F32))
    cos_s, sin_s = _rope_tables(jnp.arange(dq, dtype=F32) + past_len)
    cos_s = jnp.broadcast_to(cos_s, (db * dq, HEAD_DIM))
    sin_s = jnp.broadcast_to(sin_s, (db * dq, HEAD_DIM))

    xp = x_prompt.reshape(batch * seq, d)
    xs = x_sample.reshape(db * dq, d)
    tp = _row_tile(xp.shape[0])
    ts = _row_tile(xs.shape[0])
    xnp = _rmsnorm(xp, norm_w[0], tp)
    xns = _rmsnorm(xs, norm_w[0], ts)

    n_moba = depth // 2
    assert db == 8 and depth == 2 * n_moba

    def sides(layer):
        return tuple((cache_k, page_table, layer // 2, (layer % 2) * 4 + c) for c in range(4))

    hg_p, hg_s, ks_l, vs_l = [], [], [], []
    k_stack = v_stack = None
    sums_parts = []
    for i in range(depth):
        j = i // 2
        if i % 2 == 0:
            (q, lf, k, v, sz), part = _proj_hgrn(xnp, w_in_b, i, hgrn_lower_bounds, j, tp, sides(i))
            sums_parts = part
            yp, sp = _hgrn_scan(q, lf, k, v, sz, hgrn_gnorm[j], batch, seq)
            (q, lf, k, v, sz), _ = _proj_hgrn(xns, w_in_b, i, hgrn_lower_bounds, j, ts)
            ys, ss = _hgrn_step(q, lf, k, v, sz, hgrn_gnorm[j], state_hgrn, j)
            hg_p.append(sp)
            hg_s.append(ss)
        else:
            (q, k_stack, v_stack, sz), part = _proj_moba(
                xnp, w_in_b, i, cos_p, sin_p, tp, sides(i),
                ((j, n_moba, k_stack), (j, n_moba, v_stack)))
            yp = _moba_prompt(q, k_stack, v_stack, sz, batch, seq, slab=j)
            (q, k, v, sz), _ = _proj_moba(xns, w_in_b, i, cos_s, sin_s, ts)
            idx = _select_blocks(q, jnp.stack(sums_parts + part))
            ys = _moba_decode(q, k, v, sz, cache_k, cache_v, idx, page_table, j)
            ks_l.append(k)
            vs_l.append(v)
        last = i == depth - 1
        nw = final_norm_w if last else norm_w[i + 1]
        outp = _oproj(yp, w_out_b, i, xp, nw, tp, last)
        outs = _oproj(ys, w_out_b, i, xs, nw, ts, last)
        if last:
            y_prompt, y_sample = outp[0], outs[0]
        else:
            xp, xnp = outp
            xs, xns = outs

    shp = (n_moba, batch, seq, N_HEADS, HEAD_DIM)
    shs = (db, dq, N_HEADS, HEAD_DIM)
    return (y_prompt.reshape(batch, seq, d),
            y_sample.reshape(db, dq, d),
            jnp.stack(hg_p),
            k_stack.reshape(shp),
            v_stack.reshape(shp),
            jnp.stack(hg_s),
            jnp.stack([t.reshape(shs) for t in ks_l]),
            jnp.stack([t.reshape(shs) for t in vs_l]))
```

```python
import functools
import math

import numpy as np
import jax
import jax.numpy as jnp
from jax import lax
from jax.experimental import pallas as pl
from jax.experimental.pallas import tpu as pltpu

F32 = jnp.float32
BF16 = jnp.bfloat16

LANES = 128
SUBLANES = 8
VMEM_LIMIT_BYTES = 56 * 1024 * 1024

N_HEADS = 16
HEAD_DIM = 128
PAGE_SIZE = 128
MOBA_BLOCK = 256
MOBA_TOPK = 3
ROPE_THETA = 10000.0
RMS_EPS = 1e-6
NEG = -1e30
LB_FLOOR = 1e-30

HG_CHUNK = 128
HG_DIAG = 8
HG_LEVELS = (8, 16, 32, 64)
HG_HEADS_PER_STEP = 4
HG_SEQ_PARTS = 2
MOBA_HEADS_PER_STEP = 4


def _cparams(sem):
    return pltpu.CompilerParams(dimension_semantics=sem, vmem_limit_bytes=VMEM_LIMIT_BYTES)


def _silu(x):
    return x / (1.0 + jnp.exp(-x))


def _rms_kernel(x_ref, w_ref, o_ref):
    x = x_ref[...]
    ms = jnp.mean(x * x, axis=-1, keepdims=True)
    o_ref[...] = (x * lax.rsqrt(ms + RMS_EPS) * w_ref[...]).astype(o_ref.dtype)


def _rmsnorm(x, w, tm):
    m, d = x.shape
    return pl.pallas_call(
        _rms_kernel,
        out_shape=jax.ShapeDtypeStruct((m, d), BF16),
        grid=(m // tm,),
        in_specs=[pl.BlockSpec((tm, d), lambda i: (i, 0)),
                  pl.BlockSpec((1, d), lambda i: (0, 0))],
        out_specs=pl.BlockSpec((tm, d), lambda i: (i, 0)),
        compiler_params=_cparams(("parallel",)),
        name="rmsnorm",
    )(x, w.reshape(1, d))


def _epi_plain(acc, extra, outs):
    outs[0][...] = acc


def _epi_silu(acc, extra, outs):
    outs[0][...] = _silu(acc)


def _epi_rope(acc, extra, outs):
    cos_ref, sin_ref = extra
    cos = cos_ref[...]
    sin = sin_ref[...]
    for h in range(N_HEADS):
        xh = acc[:, h * HEAD_DIM:(h + 1) * HEAD_DIM]
        outs[0][:, h * HEAD_DIM:(h + 1) * HEAD_DIM] = (
            xh * cos + pltpu.roll(xh, HEAD_DIM // 2, 1) * sin)


def _hgrn_lower_bound(lbw, layer):
    n = lbw.shape[0]
    rows = [lbw[r:r + 1, :] for r in range(n)]
    mx = rows[0]
    for r in rows[1:]:
        mx = jnp.maximum(mx, r)
    es = [jnp.exp(r - mx) for r in rows]
    tot = es[0]
    for e in es[1:]:
        tot = tot + e
    sm = [e / tot for e in es]
    cs = sm[0]
    for r in range(1, layer + 1):
        cs = cs + sm[r]
    return cs - sm[0]


def _epi_gates(acc, extra, outs, *, layer):
    lb = _hgrn_lower_bound(extra[0][...], layer)
    lb_floor = jnp.maximum(lb, LB_FLOOR)
    one_m = 1.0 - lb
    z = acc
    t = jnp.exp(-jnp.abs(z))
    r = 1.0 / (1.0 + t)
    pos = z >= 0.0
    sig = jnp.where(pos, r, t * r)
    nsig = jnp.where(pos, t * r, r)
    outs[0][...] = jnp.log2(lb_floor + one_m * sig)
    outs[1][...] = one_m * nsig


def _proj_kernel(*refs, epilogue, n_extra, n_prefetch, side_kind, n_side, item0, has_prev):
    refs = refs[n_prefetch:]
    x_ref, w_ref = refs[0], refs[1]
    extra = refs[2:2 + n_extra]
    pos = 2 + n_extra
    side_in = refs[pos:pos + n_side]
    pos += n_side + (1 if has_prev else 0)
    outs = refs[pos:]
    acc = jnp.dot(x_ref[...], w_ref[...], preferred_element_type=F32)
    if side_kind is not None:
        outs, side_out = outs[:-1], outs[-1]
    if side_kind == "sums":
        ppb = MOBA_BLOCK // PAGE_SIZE
        for u in range(n_side // ppb):
            s = jnp.sum(side_in[u * ppb][...], axis=0)
            for r in range(1, ppb):
                s = s + jnp.sum(side_in[u * ppb + r][...], axis=0)
            side_out[u] = s
    if side_kind == "attend":
        q_ref, kn_ref, vn_ref, sz_ref = side_in[:4]
        n_pg = MOBA_TOPK * (MOBA_BLOCK // PAGE_SIZE)
        n_items = (n_side - 4) // (2 * n_pg)
        for it in range(n_items):
            local = pl.program_id(0) * n_items + it
            item = item0 + local
            row = pl.ds(item, 1)
            pages = side_in[4 + it * 2 * n_pg:4 + (it + 1) * 2 * n_pg]
            side_out[pl.ds(local, 1), :] = _decode_item(
                q_ref[row, :], kn_ref[row, :], vn_ref[row, :], sz_ref[row, :], item % SUBLANES,
                pages[:n_pg], pages[n_pg:])
    epilogue(acc, extra, outs)


def _proj(xn, w, layer, group, epilogue, extra, extra_specs, n_out, tm, name, side=None, stack=None):
    m, d = xn.shape
    steps = m // tm
    operands = [xn, w, *extra]
    in_specs = [pl.BlockSpec((tm, d), lambda i, *_: (i, 0)),
                pl.BlockSpec((None, d, d), lambda i, *_: (layer, 0, group))] + list(extra_specs)
    out_shape = [jax.ShapeDtypeStruct((m, d), F32)] * n_out
    out_specs = [pl.BlockSpec((tm, d), lambda i, *_: (i, 0))] * n_out
    prefetch, n_side, aliases, side_kind, item0 = [], 0, {}, None, 0
    ppb = MOBA_BLOCK // PAGE_SIZE
    if side is not None and side[0] == "sums":
        side_kind, cache, page_table, moba_layer, page0, n_call = side
        n_side = n_call // steps
        assert n_side * steps == n_call and n_side % ppb == 0
        prefetch = [page_table.reshape(-1)]

        def page_spec(u):
            return pl.BlockSpec(
                (None, None, PAGE_SIZE, N_HEADS, HEAD_DIM),
                lambda i, pt: (moba_layer,
                               pt[page0 + jnp.minimum(i, steps - 1) * n_side + u], 0, 0, 0))

        operands += [cache] * n_side
        in_specs += [page_spec(u) for u in range(n_side)]
        out_shape = out_shape + [jax.ShapeDtypeStruct((n_call // ppb, N_HEADS, HEAD_DIM), F32)]
        out_specs = out_specs + [pl.BlockSpec((n_side // ppb, N_HEADS, HEAD_DIM),
                                              lambda i, *_: (i, 0, 0))]
    if side is not None and side[0] == "attend":
        side_kind, cache_k, cache_v, page_table, idx, moba_layer, rows, item0, n_call = side
        n_pages = page_table.shape[1]
        n_pg = MOBA_TOPK * ppb
        n_items = n_call // steps
        assert n_items * steps == n_call
        prefetch = [page_table.reshape(-1), idx.reshape(-1)]

        def kv_spec(it, t):
            def page_map(i, pt, ix):
                item = item0 + jnp.minimum(i, steps - 1) * n_items + it
                blk = ix[item * MOBA_TOPK + t // ppb]
                page = pt[(item // N_HEADS) * n_pages + blk * ppb + t % ppb]
                return (moba_layer, page, 0, (item % N_HEADS) // SUBLANES, 0)
            return pl.BlockSpec((None, None, PAGE_SIZE, SUBLANES, HEAD_DIM), page_map)

        whole = pl.BlockSpec(rows[0].shape, lambda i, *_: (0, 0))
        operands += list(rows)
        in_specs += [whole] * len(rows)
        for it in range(n_items):
            pages = [kv_spec(it, t) for t in range(n_pg)]
            operands += [cache_k] * n_pg + [cache_v] * n_pg
            in_specs += pages + pages
        n_side = len(rows) + n_items * 2 * n_pg
        out_shape = out_shape + [jax.ShapeDtypeStruct((n_call, HEAD_DIM), F32)]
        out_specs = out_specs + [pl.BlockSpec((n_call, HEAD_DIM), lambda i, *_: (0, 0))]
    if stack is not None:
        slab, n_slabs, prev = stack
        out_shape = [jax.ShapeDtypeStruct((n_slabs, m, d), F32)] + out_shape[1:]
        out_specs = [pl.BlockSpec((None, tm, d), lambda i, *_: (slab, i, 0))] + out_specs[1:]
        if prev is not None:
            aliases = {len(prefetch) + len(operands): 0}
            operands.append(prev)
            in_specs.append(pl.BlockSpec(memory_space=pl.ANY))
    kern = functools.partial(_proj_kernel, epilogue=epilogue, n_extra=len(extra),
                             n_prefetch=len(prefetch), side_kind=side_kind, n_side=n_side,
                             item0=item0, has_prev=stack is not None and stack[2] is not None)
    return pl.pallas_call(
        kern,
        out_shape=out_shape,
        grid_spec=pltpu.PrefetchScalarGridSpec(
            num_scalar_prefetch=len(prefetch), grid=(steps,), in_specs=in_specs,
            out_specs=out_specs),
        input_output_aliases=aliases,
        compiler_params=_cparams(("arbitrary" if side_kind == "attend" else "parallel",)),
        name=name,
    )(*prefetch, *operands)


def _rope_tables(pos):
    half = HEAD_DIM // 2
    inv = jnp.power(ROPE_THETA, -jnp.arange(half, dtype=F32) / half)
    ang = pos[:, None] * inv[None, :]
    cos = jnp.cos(ang)
    sin = jnp.sin(ang)
    return jnp.concatenate([cos, cos], axis=-1), jnp.concatenate([-sin, sin], axis=-1)


def _proj_hgrn(xn, w, layer, lbw, hg_layer, tm, sides=(None,) * 4):
    q, *s0 = _proj(xn, w, layer, 0, _epi_silu, (), (), 1, tm, "proj_hgrn_q", side=sides[0])
    lf, k, *s1 = _proj(xn, w, layer, 1, functools.partial(_epi_gates, layer=hg_layer), (lbw,),
                       (pl.BlockSpec(lbw.shape, lambda i, *_: (0, 0)),), 2, tm, "proj_hgrn_f",
                       side=sides[1])
    v, *s2 = _proj(xn, w, layer, 2, _epi_plain, (), (), 1, tm, "proj_hgrn_i", side=sides[2])
    sz, *s3 = _proj(xn, w, layer, 3, _epi_silu, (), (), 1, tm, "proj_z", side=sides[3])
    return (q, lf, k, v, sz), s0 + s1 + s2 + s3


def _proj_moba(xn, w, layer, cos, sin, tm, sides=(None,) * 4, stacks=(None, None)):
    nblk = cos.shape[0] // tm
    tspec = pl.BlockSpec((tm, HEAD_DIM), lambda i, *_: (i % nblk, 0))
    q, *s0 = _proj(xn, w, layer, 0, _epi_rope, (cos, sin), (tspec, tspec), 1, tm, "proj_moba_q",
                   side=sides[0])
    k, *s1 = _proj(xn, w, layer, 1, _epi_rope, (cos, sin), (tspec, tspec), 1, tm, "proj_moba_k",
                   side=sides[1], stack=stacks[0])
    v, *s2 = _proj(xn, w, layer, 2, _epi_plain, (), (), 1, tm, "proj_moba_v", side=sides[2],
                   stack=stacks[1])
    sz, *s3 = _proj(xn, w, layer, 3, _epi_silu, (), (), 1, tm, "proj_z", side=sides[3])
    return (q, k, v, sz), s0 + s1 + s2 + s3


def _oproj_kernel(y_ref, w_ref, x_ref, nw_ref, *outs, emit_x):
    xo = x_ref[...] + jnp.dot(y_ref[...].astype(BF16), w_ref[...], preferred_element_type=F32)
    ms = jnp.mean(xo * xo, axis=-1, keepdims=True)
    xn = xo * lax.rsqrt(ms + RMS_EPS) * nw_ref[...]
    if emit_x:
        outs[0][...] = xo
        outs[1][...] = xn.astype(outs[1].dtype)
    else:
        outs[0][...] = xn.astype(outs[0].dtype)


def _oproj(y, w, layer, x, nw, tm, last):
    m, d = x.shape
    row = pl.BlockSpec((tm, d), lambda i: (i, 0))
    if last:
        out_shape = [jax.ShapeDtypeStruct((m, d), F32)]
    else:
        out_shape = [jax.ShapeDtypeStruct((m, d), F32), jax.ShapeDtypeStruct((m, d), BF16)]
    return pl.pallas_call(
        functools.partial(_oproj_kernel, emit_x=not last),
        out_shape=out_shape,
        grid=(m // tm,),
        in_specs=[row, pl.BlockSpec((None, d, d), lambda i: (layer, 0, 0)), row,
                  pl.BlockSpec((1, d), lambda i: (0, 0))],
        out_specs=[row] * len(out_shape),
        compiler_params=_cparams(("parallel",)),
        name="oproj",
    )(y, w, x, nw.reshape(1, d))


def _hgrn_level_masks():
    c = HG_CHUNK
    t = np.arange(c)[:, None]
    s = np.arange(c)[None, :]
    masks = []
    for m in HG_LEVELS:
        masks.append(((t // (2 * m) == s // (2 * m)) & (t % (2 * m) >= m) & (s % (2 * m) < m)))
    return np.stack(masks).astype(np.float32)


def _split3(x):
    hi = x.astype(BF16)
    r1 = x - hi.astype(F32)
    mid = r1.astype(BF16)
    lo = (r1 - mid.astype(F32)).astype(BF16)
    return hi, mid, lo


def _hgrn_chunks(qs, lfs, ks, vs, sts, tri3, mask_ref, place_ref, bk_ref):
    c = HG_CHUNK
    heads = range(len(qs))
    nt = (((1,), (1,)), ((), ()))
    bs = [jnp.dot(tri3, jnp.concatenate(_split3(lfs[h]), axis=0), preferred_element_type=F32)
          for h in heads]
    accs = [jnp.zeros((c, c), F32) for _ in heads]
    for li, m in enumerate(HG_LEVELS):
        ams = []
        for h in heads:
            b = bs[h]
            pieces = []
            for blk in range(c // (2 * m)):
                bnd = blk * 2 * m + m - 1
                pieces.append(jnp.broadcast_to(b[bnd:bnd + 1, :], (2 * m, HEAD_DIM)))
            ref_b = pieces[0] if len(pieces) == 1 else jnp.concatenate(pieces, axis=0)
            d = b - ref_b
            e = jnp.exp2(jnp.minimum(d, -d))
            ams.append(lax.dot_general((qs[h] * e).astype(BF16), (ks[h] * e).astype(BF16), nt,
                                       preferred_element_type=F32))
        for h in heads:
            accs[h] = accs[h] + ams[h] * mask_ref[li]
    for h in heads:
        b = bs[h]
        bk_ref[h, 0] = b
        bk_ref[h, 1] = ks[h]
        rows = []
        for g in range(c // HG_DIAG):
            g0 = g * HG_DIAG
            qg = qs[h][g0:g0 + HG_DIAG, :]
            bg = b[g0:g0 + HG_DIAG, :]
            ag = jnp.zeros((HG_DIAG, c), F32)
            for s in range(HG_DIAG):
                b_s = bk_ref[h, 0, g0 + s:g0 + s + 1, :]
                k_s = bk_ref[h, 1, g0 + s:g0 + s + 1, :]
                w = qg * jnp.exp2(jnp.minimum(bg - b_s, 0.0)) * k_s
                col = jnp.sum(w, axis=-1, keepdims=True)
                ag = jnp.where(place_ref[g * HG_DIAG + s] > 0.0, col, ag)
            rows.append(ag)
        accs[h] = accs[h] + jnp.concatenate(rows, axis=0)
    os = [jnp.dot(accs[h].astype(BF16), vs[h].astype(BF16), preferred_element_type=F32)
          + lax.dot_general((qs[h] * jnp.exp2(bs[h])).astype(BF16), sts[h].astype(BF16), nt,
                            preferred_element_type=F32) for h in heads]
    new = []
    for h in heads:
        bl = bs[h][c - 1:c, :]
        kd = (ks[h] * jnp.exp2(bl - bs[h])).astype(BF16)
        new.append(sts[h] * jnp.exp2(bl)
                   + jnp.dot(vs[h].T.astype(BF16), kd, preferred_element_type=F32))
    return os, new


def _hgrn_scan_kernel(q_ref, lf_ref, k_ref, v_ref, sz_ref, g_ref, tri_ref, mask_ref, place_ref,
                      y_ref, s_ref, st_ref, bk_ref):
    c = HG_CHUNK
    n_chunks = q_ref.shape[0] // c
    part = pl.program_id(2)

    @pl.when(part == 0)
    def _():
        st_ref[...] = jnp.zeros_like(st_ref)

    cols = [slice(h * HEAD_DIM, (h + 1) * HEAD_DIM) for h in range(HG_HEADS_PER_STEP)]

    def chunk(ci, carry):
        r0 = pl.multiple_of(ci * c, c)
        rs = pl.ds(r0, c)
        os, new = _hgrn_chunks([q_ref[rs, cs] for cs in cols], [lf_ref[rs, cs] for cs in cols],
                               [k_ref[rs, cs] for cs in cols], [v_ref[rs, cs] for cs in cols],
                               [st_ref[h] for h in range(len(cols))], tri_ref[...],
                               mask_ref, place_ref, bk_ref)
        for h, cs in enumerate(cols):
            st_ref[h] = new[h]
            o = os[h]
            ms = jnp.mean(o * o, axis=-1, keepdims=True)
            y = o * lax.rsqrt(ms + RMS_EPS) * g_ref[...] * sz_ref[rs, cs]
            y_ref[rs, cs] = y.astype(y_ref.dtype)
        return carry

    lax.fori_loop(0, n_chunks, chunk, 0)

    @pl.when(part == pl.num_programs(2) - 1)
    def _():
        for h in range(HG_HEADS_PER_STEP):
            s_ref[h] = st_ref[h].T


def _hgrn_place_masks():
    c = HG_CHUNK
    t = np.arange(HG_DIAG)[:, None]
    lane = np.arange(c)[None, :]
    out = [((lane == g * HG_DIAG + s) & (t >= s)) for g in range(c // HG_DIAG) for s in range(HG_DIAG)]
    return np.stack(out).astype(np.float32)


def _hgrn_scan(q, lf, k, v, sz, gnorm, batch, seq):
    m, d = q.shape
    hp = HG_HEADS_PER_STEP
    tri = jnp.asarray(np.tile(np.tril(np.ones((HG_CHUNK, HG_CHUNK), np.float32)), (1, 3)), BF16)
    masks = jnp.asarray(_hgrn_level_masks())
    place = jnp.asarray(_hgrn_place_masks())
    parts = HG_SEQ_PARTS
    assert seq % (parts * HG_CHUNK) == 0
    col = pl.BlockSpec((seq // parts, hp * HEAD_DIM), lambda b, h, p: (b * parts + p, h))
    const = lambda a: pl.BlockSpec(a.shape, lambda b, h, p: (0,) * a.ndim)
    y, s = pl.pallas_call(
        _hgrn_scan_kernel,
        out_shape=[jax.ShapeDtypeStruct((m, d), BF16),
                   jax.ShapeDtypeStruct((batch, N_HEADS, HEAD_DIM, HEAD_DIM), F32)],
        grid=(batch, N_HEADS // hp, parts),
        in_specs=[col, col, col, col, col,
                  pl.BlockSpec((1, HEAD_DIM), lambda b, h, p: (0, 0)),
                  const(tri), const(masks), const(place)],
        out_specs=[col, pl.BlockSpec((None, hp, HEAD_DIM, HEAD_DIM),
                                     lambda b, h, p: (b, h, 0, 0))],
        scratch_shapes=[pltpu.VMEM((hp, HEAD_DIM, HEAD_DIM), F32),
                        pltpu.VMEM((hp, 2, HG_CHUNK, HEAD_DIM), F32)],
        compiler_params=_cparams(("parallel", "parallel", "arbitrary")),
        name="hgrn_scan",
    )(q, lf, k, v, sz, gnorm.reshape(1, HEAD_DIM), tri, masks, place)
    return y, s


def _lane_bcast_col(rowvec):
    return jnp.broadcast_to(rowvec, (HEAD_DIM, HEAD_DIM)).T


def _hgrn_step_kernel(q_ref, lf_ref, k_ref, v_ref, sz_ref, g_ref, s0_ref, y_ref, s_ref):
    b = pl.program_id(0)
    lf, k, q, v, sz = lf_ref[b], k_ref[b], q_ref[b], v_ref[b], sz_ref[b]
    for h in range(N_HEADS):
        hs = slice(h, h + 1)
        f_col = _lane_bcast_col(jnp.exp2(lf[hs]))
        k_col = _lane_bcast_col(k[hs])
        q_col = _lane_bcast_col(q[hs])
        s_new = f_col * s0_ref[h] + k_col * v[hs]
        s_ref[h] = s_new
        o = jnp.sum(q_col * s_new, axis=0, keepdims=True)
        ms = jnp.mean(o * o, axis=-1, keepdims=True)
        y_ref[b, hs, :] = o * lax.rsqrt(ms + RMS_EPS) * g_ref[...] * sz[hs]


def _hgrn_step(q, lf, k, v, sz, gnorm, state_all, layer):
    db, d = q.shape
    whole = pl.BlockSpec((db, N_HEADS, HEAD_DIM), lambda b: (0, 0, 0))
    q, lf, k, v, sz = (a.reshape(db, N_HEADS, HEAD_DIM) for a in (q, lf, k, v, sz))
    y, s = pl.pallas_call(
        _hgrn_step_kernel,
        out_shape=[jax.ShapeDtypeStruct((db, N_HEADS, HEAD_DIM), F32),
                   jax.ShapeDtypeStruct((db, N_HEADS, HEAD_DIM, HEAD_DIM), F32)],
        grid=(db,),
        in_specs=[whole, whole, whole, whole, whole,
                  pl.BlockSpec((1, HEAD_DIM), lambda b: (0, 0)),
                  pl.BlockSpec((None, None, N_HEADS, HEAD_DIM, HEAD_DIM),
                               lambda b: (layer, b, 0, 0, 0))],
        out_specs=[whole, pl.BlockSpec((None, N_HEADS, HEAD_DIM, HEAD_DIM),
                                       lambda b: (b, 0, 0, 0))],
        compiler_params=_cparams(("arbitrary",)),
        name="hgrn_step",
    )(q, lf, k, v, sz, gnorm.reshape(1, HEAD_DIM), state_all)
    return y.reshape(db, d), s


def _moba_prompt_kernel(q_ref, k_ref, v_ref, sz_ref, y_ref, means_ref, kb_ref, vt_ref, qb_ref,
                        bias_ref, m_ref, l_ref, acc_ref):
    blk = MOBA_BLOCK
    nb = k_ref.shape[0] // blk
    i = pl.program_id(2)
    scale = HEAD_DIM ** -0.5 * math.log2(math.e)
    heads = [slice(h * HEAD_DIM, (h + 1) * HEAD_DIM) for h in range(MOBA_HEADS_PER_STEP)]
    nt = (((1,), (1,)), ((), ()))

    @pl.when(i == 0)
    def _():
        for h, cs in enumerate(heads):
            for n in range(nb):
                kb = k_ref[n * blk:(n + 1) * blk, cs]
                means_ref[h, n:n + 1, :] = jnp.sum(kb, axis=0, keepdims=True) * (1.0 / blk)
                kb_ref[h, n] = kb.astype(BF16)
                vt_ref[h, n] = v_ref[n * blk:(n + 1) * blk, cs].T.astype(BF16)

    nidx = lax.broadcasted_iota(jnp.int32, (nb, blk), 0)
    past = nidx < i
    causal = (lax.broadcasted_iota(jnp.int32, (blk, blk), 0)
              <= lax.broadcasted_iota(jnp.int32, (blk, blk), 1))
    qs = [q_ref[:, cs] for cs in heads]
    sts = [lax.dot_general(means_ref[h], qs[h], nt, precision=lax.Precision.HIGHEST,
                           preferred_element_type=F32) for h in range(len(heads))]
    for h in range(len(heads)):
        qb_ref[h] = (qs[h] * scale).astype(BF16)
    ss = [lax.dot_general(kb_ref[h, i], qb_ref[h], nt, preferred_element_type=F32)
          for h in range(len(heads))]
    for h in range(len(heads)):
        sm = jnp.where(past, sts[h], NEG)
        rank = jnp.zeros((nb, blk), F32)
        for m in range(nb):
            rm = sm[m:m + 1, :]
            beats = (rm > sm) | ((rm == sm) & (m < nidx))
            rank = rank + jnp.where(beats, 1.0, 0.0)
        bias_ref[h] = jnp.where(past & (rank < float(MOBA_TOPK)), 0.0, NEG)
    ps = []
    for h in range(len(heads)):
        s = jnp.where(causal, ss[h], NEG)
        m0 = jnp.max(s, axis=0, keepdims=True)
        p = jnp.exp2(s - m0)
        m_ref[h] = m0
        l_ref[h] = jnp.sum(p, axis=0, keepdims=True)
        ps.append(p.astype(BF16))
    for h in range(len(heads)):
        acc_ref[h] = jnp.dot(vt_ref[h, i], ps[h], preferred_element_type=F32)

    def kv_block(n, carry):
        hs = range(MOBA_HEADS_PER_STEP)
        ss = [lax.dot_general(kb_ref[h, n], qb_ref[h], nt, preferred_element_type=F32)
              + bias_ref[h, pl.ds(n, 1), :] for h in hs]
        ps, alphas = [], []
        for h in hs:
            m_old = m_ref[h]
            m_new = jnp.maximum(m_old, jnp.max(ss[h], axis=0, keepdims=True))
            alpha = jnp.exp2(m_old - m_new)
            p = jnp.exp2(ss[h] - m_new)
            l_ref[h] = alpha * l_ref[h] + jnp.sum(p, axis=0, keepdims=True)
            m_ref[h] = m_new
            ps.append(p.astype(BF16))
            alphas.append(alpha)
        pvs = [jnp.dot(vt_ref[h, n], ps[h], preferred_element_type=F32) for h in hs]
        for h in hs:
            acc_ref[h] = alphas[h] * acc_ref[h] + pvs[h]
        return carry

    lax.fori_loop(0, i, kv_block, 0)
    for h, cs in enumerate(heads):
        o = (acc_ref[h] / l_ref[h]).T
        y_ref[:, cs] = (o * sz_ref[:, cs]).astype(y_ref.dtype)


def _moba_prompt(q, k, v, sz, batch, seq, slab=None):
    m, d = q.shape
    hp = MOBA_HEADS_PER_STEP
    nq = seq // MOBA_BLOCK
    nb = seq // MOBA_BLOCK
    qspec = pl.BlockSpec((MOBA_BLOCK, hp * HEAD_DIM), lambda b, h, i: (b * nq + i, h))
    if slab is None:
        kvspec = pl.BlockSpec((seq, hp * HEAD_DIM), lambda b, h, i: (b, h))
    else:
        kvspec = pl.BlockSpec((None, seq, hp * HEAD_DIM), lambda b, h, i: (slab, b, h))
    return pl.pallas_call(
        _moba_prompt_kernel,
        out_shape=jax.ShapeDtypeStruct((m, d), BF16),
        grid=(batch, N_HEADS // hp, nq),
        in_specs=[qspec, kvspec, kvspec, qspec],
        out_specs=qspec,
        scratch_shapes=[pltpu.VMEM((hp, nb, HEAD_DIM), F32),
                        pltpu.VMEM((hp, nb, MOBA_BLOCK, HEAD_DIM), BF16),
                        pltpu.VMEM((hp, nb, HEAD_DIM, MOBA_BLOCK), BF16),
                        pltpu.VMEM((hp, MOBA_BLOCK, HEAD_DIM), BF16),
                        pltpu.VMEM((hp, nb, MOBA_BLOCK), F32),
                        pltpu.VMEM((hp, 1, MOBA_BLOCK), F32),
                        pltpu.VMEM((hp, 1, MOBA_BLOCK), F32),
                        pltpu.VMEM((hp, HEAD_DIM, MOBA_BLOCK), F32)],
        compiler_params=_cparams(("parallel", "parallel", "arbitrary")),
        name="moba_prompt",
    )(q, k, v, sz)


def _select_kernel(q_ref, sums_ref, idx_ref):
    nblk = sums_ref.shape[0]
    lane = lax.broadcasted_iota(jnp.int32, (nblk, LANES), 1)
    sc = jnp.full((nblk, LANES), NEG, F32)
    for h in range(N_HEADS):
        prod = sums_ref[:, h, :] * (1.0 / MOBA_BLOCK) * q_ref[h:h + 1, :]
        col = jnp.sum(prod, axis=-1, keepdims=True)
        sc = jnp.where(lane == h, col, sc)
    nidx = lax.broadcasted_iota(jnp.int32, (nblk, LANES), 0).astype(F32)
    out = jnp.zeros((SUBLANES, LANES), F32)
    orow = lax.broadcasted_iota(jnp.int32, (SUBLANES, LANES), 0)
    for t in range(MOBA_TOPK):
        mx = jnp.max(sc, axis=0, keepdims=True)
        pick = jnp.min(jnp.where(sc == mx, nidx, float(nblk)), axis=0, keepdims=True)
        out = jnp.where(orow == t, pick, out)
        sc = jnp.where(nidx == pick, -jnp.inf, sc)
    idx_ref[...] = out.astype(jnp.int32)


def _select_blocks(q, sums):
    db, d = q.shape
    nblk = sums.shape[1]
    idx = pl.pallas_call(
        _select_kernel,
        out_shape=jax.ShapeDtypeStruct((db, SUBLANES, LANES), jnp.int32),
        grid=(db,),
        in_specs=[pl.BlockSpec((None, N_HEADS, HEAD_DIM), lambda b: (b, 0, 0)),
                  pl.BlockSpec((None, nblk, N_HEADS, HEAD_DIM), lambda b: (b, 0, 0, 0))],
        out_specs=pl.BlockSpec((None, SUBLANES, LANES), lambda b: (b, 0, 0)),
        compiler_params=_cparams(("parallel",)),
        name="moba_select",
    )(q.reshape(db, N_HEADS, HEAD_DIM), sums)
    return jnp.transpose(idx[:, :MOBA_TOPK, :N_HEADS], (0, 2, 1))


def _decode_item(q, k_new, v_new, sz, hsub, k_refs, v_refs):
    scale = HEAD_DIM ** -0.5
    mine = lax.broadcasted_iota(jnp.int32, (SUBLANES, HEAD_DIM), 0) == hsub
    mine_col = lax.broadcasted_iota(jnp.int32, (1, SUBLANES, 1), 1) == hsub
    s_self = (jnp.sum(q * k_new, axis=-1, keepdims=True) * scale)[None]
    scores = []
    mx8 = jnp.full((1, SUBLANES, 1), -jnp.inf, F32)
    for kp in k_refs:
        s = jnp.sum(kp[...] * q[None], axis=2, keepdims=True) * scale
        s = jnp.where(mine_col, s, -jnp.inf)
        scores.append(s)
        mx8 = jnp.maximum(mx8, jnp.max(s, axis=0, keepdims=True))
    mx = jnp.maximum(s_self, jnp.max(mx8, axis=1, keepdims=True))
    p_self = jnp.exp(s_self - mx)
    den8 = jnp.zeros((1, SUBLANES, 1), F32)
    acc = jnp.where(mine, v_new, 0.0) * p_self[0]
    for s, vp in zip(scores, v_refs):
        p = jnp.exp(s - mx)
        den8 = den8 + jnp.sum(p, axis=0, keepdims=True)
        acc = acc + jnp.sum(p * vp[...], axis=0)
    den = p_self + jnp.sum(den8, axis=1, keepdims=True)
    o = jnp.sum(jnp.where(mine, acc, 0.0), axis=0, keepdims=True)
    return o / den[0] * sz


def _row_tile(m):
    return 256 if m % 256 == 0 else m


def kernel(x_prompt, x_sample, cache_k, cache_v, state_hgrn, page_table, norm_w, w_in, w_out,
           hgrn_lower_bounds, hgrn_gnorm, final_norm_w):
    batch, seq, d = x_prompt.shape
    db, dq, _ = x_sample.shape
    depth = norm_w.shape[0]
    past_len = page_table.shape[1] * PAGE_SIZE
    assert dq == 1 and d == N_HEADS * HEAD_DIM and seq % MOBA_BLOCK == 0 and seq % HG_CHUNK == 0
    assert past_len % MOBA_BLOCK == 0 and past_len // MOBA_BLOCK >= MOBA_TOPK

    w_in_b = w_in.astype(BF16)
    w_out_b = w_out.astype(BF16)
    cos_p, sin_p = _rope_tables(jnp.arange(seq, dtype=F32))
    cos_s, sin_s = _rope_tables(jnp.arange(dq, dtype=F32) + past_len)
    cos_s = jnp.broadcast_to(cos_s, (db * dq, HEAD_DIM))
    sin_s = jnp.broadcast_to(sin_s, (db * dq, HEAD_DIM))

    xp = x_prompt.reshape(batch * seq, d)
    xs = x_sample.reshape(db * dq, d)
    tp = _row_tile(xp.shape[0])
    ts = _row_tile(xs.shape[0])
    xnp = _rmsnorm(xp, norm_w[0], tp)
    xns = _rmsnorm(xs, norm_w[0], ts)

    n_moba = depth // 2
    n_pages = page_table.shape[1]
    assert depth == 2 * n_moba and (db * N_HEADS) % 4 == 0 and (db * n_pages) % 4 == 0
    per_call = db * n_pages // 4
    items_per_call = db * N_HEADS // 4
    rows = lambda a: a.reshape(db * N_HEADS, HEAD_DIM)

    hg_p, hg_s, ks_l, vs_l = [], [], [], []
    k_stack = v_stack = None
    block_sums = {}
    sample = {"x": xs, "xn": xns, "layer": 0, "y_final": None}

    def sample_oproj(i, ys):
        last = i == depth - 1
        nw = final_norm_w if last else norm_w[i + 1]
        out = _oproj(ys, w_out_b, i, sample["x"], nw, ts, last)
        if last:
            sample["y_final"] = out[0]
        else:
            sample["x"], sample["xn"] = out
        sample["layer"] = i + 1

    def sample_until_attend(j):
        i_moba = 2 * j + 1
        while sample["layer"] < i_moba:
            i = sample["layer"]
            assert i % 2 == 0
            (q, lf, k, v, sz), _ = _proj_hgrn(sample["xn"], w_in_b, i, hgrn_lower_bounds, i // 2, ts)
            ys, ss = _hgrn_step(q, lf, k, v, sz, hgrn_gnorm[i // 2], state_hgrn, i // 2)
            hg_s.append(ss)
            sample_oproj(i, ys)
        (q, k, v, sz), _ = _proj_moba(sample["xn"], w_in_b, i_moba, cos_s, sin_s, ts)
        ks_l.append(k)
        vs_l.append(v)
        idx = _select_blocks(q, block_sums[j])
        return tuple(("attend", cache_k, cache_v, page_table, idx, j,
                      (rows(q), rows(k), rows(v), rows(sz)), c * items_per_call, items_per_call)
                     for c in range(4))

    for i in range(depth):
        j = i // 2
        if i < n_moba:
            sides = tuple(("sums", cache_k, page_table, i, c * per_call, per_call) for c in range(4))
        else:
            sides = sample_until_attend(i - n_moba)
        if i % 2 == 0:
            (q, lf, k, v, sz), part = _proj_hgrn(xnp, w_in_b, i, hgrn_lower_bounds, j, tp, sides)
            yp, sp = _hgrn_scan(q, lf, k, v, sz, hgrn_gnorm[j], batch, seq)
            hg_p.append(sp)
        else:
            (q, k_stack, v_stack, sz), part = _proj_moba(
                xnp, w_in_b, i, cos_p, sin_p, tp, sides,
                ((j, n_moba, k_stack), (j, n_moba, v_stack)))
            yp = _moba_prompt(q, k_stack, v_stack, sz, batch, seq, slab=j)
        if i < n_moba:
            block_sums[i] = jnp.concatenate(part).reshape(db, n_pages // 2, N_HEADS, HEAD_DIM)
        else:
            sample_oproj(2 * (i - n_moba) + 1, jnp.concatenate(part).reshape(db, d))
        last = i == depth - 1
        nw = final_norm_w if last else norm_w[i + 1]
        outp = _oproj(yp, w_out_b, i, xp, nw, tp, last)
        if last:
            y_prompt = outp[0]
        else:
            xp, xnp = outp
    assert sample["layer"] == depth
    y_sample = sample["y_final"]

    shp = (n_moba, batch, seq, N_HEADS, HEAD_DIM)
    shs = (db, dq, N_HEADS, HEAD_DIM)
    return (y_prompt.reshape(batch, seq, d),
            y_sample.reshape(db, dq, d),
            jnp.stack(hg_p),
            k_stack.reshape(shp),
            v_stack.reshape(shp),
            jnp.stack(hg_s),
            jnp.stack([t.reshape(shs) for t in ks_l]),
            jnp.stack([t.reshape(shs) for t in vs_l]))
```
